```python
import jax
import jax.numpy as jnp
from jax import lax
import numpy as np

D_MODEL = 1024
BATCH = 16
SEQ = 256
DEPTH = 1
DEC_BATCH = 8
DEC_SEQ = 1024
PAST_LEN = 256

GRID_W = 64
HEAD_DIM = 64
N_HEADS_A = D_MODEL // 2 // HEAD_DIM
N_KV_A = N_HEADS_A // 4
N_HEADS_B = D_MODEL // 2 // HEAD_DIM
N_KV_B = N_HEADS_B // 4
Q_A_W = N_HEADS_A * HEAD_DIM
KV_A_W = N_KV_A * HEAD_DIM
Q_B_W = N_HEADS_B * HEAD_DIM
KV_B_W = N_KV_B * HEAD_DIM
IN_W = Q_A_W + 2 * KV_A_W + Q_B_W + 2 * KV_B_W
MIX_W = Q_A_W + Q_B_W
BLOCK = 128
WINDOW = 128
ROPE_THETA = 10000.0
N_EXPERTS = 16
CAPACITY_FACTOR = 2
D_FF_EXPERT = D_MODEL
N_MOD = 6
EPS = 1e-6
NEG_INF = -1e30

kernel_name = 'hybrid_dit_prefix_gqa_swa_ec_moe_step'


def rms_norm(x, g):
    xf = x.astype(jnp.float32)
    y = xf * lax.rsqrt(jnp.mean(xf * xf, axis=-1, keepdims=True) + EPS)
    return y.astype(x.dtype) * g


def _rope_1d(x, pos):
    half = x.shape[-1] // 2
    freqs = ROPE_THETA ** (-jnp.arange(half, dtype=jnp.float32) / half)
    ang = pos[:, None] * freqs[None, :]
    cos = jnp.cos(ang)[:, None, :].astype(x.dtype)
    sin = jnp.sin(ang)[:, None, :].astype(x.dtype)
    x1, x2 = x[..., :half], x[..., half:]
    return jnp.concatenate([x1 * cos - x2 * sin, x2 * cos + x1 * sin], axis=-1)


def rope_2d(x, rows, cols):
    half = x.shape[-1] // 2
    return jnp.concatenate([_rope_1d(x[..., :half], rows), _rope_1d(x[..., half:], cols)], axis=-1)


def grid_positions(n_tokens):
    rows_n = n_tokens // GRID_W
    rows = jnp.repeat(jnp.arange(rows_n, dtype=jnp.float32), GRID_W)
    cols = jnp.tile(jnp.arange(GRID_W, dtype=jnp.float32), rows_n)
    return rows, cols


def softmax_with_sink(s, sink):
    m = jnp.maximum(jnp.max(s, axis=-1, keepdims=True), sink)
    e = jnp.exp(s - m)
    return e / (jnp.sum(e, axis=-1, keepdims=True) + jnp.exp(sink - m))


def modulation(cond, w_ada, b_ada):
    m = jnp.einsum('btd,dk->btk', jax.nn.silu(cond), w_ada) + b_ada
    return jnp.split(m, N_MOD, axis=-1)


def modulate(x, g, shift, scale):
    return rms_norm(x, g) * (1.0 + scale) + shift


def attn_projections(h, w_in, g_qa, g_ka):
    B, T, _ = h.shape
    p = jnp.einsum('btd,dk->btk', h, w_in)
    c1 = Q_A_W
    c2 = c1 + KV_A_W
    c3 = c2 + KV_A_W
    c4 = c3 + Q_B_W
    c5 = c4 + KV_B_W
    qa, ka, va, qb, kb, vb = jnp.split(p, [c1, c2, c3, c4, c5], axis=-1)
    heads = lambda t: t.reshape(B, T, -1, HEAD_DIM)
    qa = rms_norm(heads(qa), g_qa)
    ka = rms_norm(heads(ka), g_ka)
    return qa, ka, heads(va), heads(qb), heads(kb), heads(vb)


def blocked_gqa(q, k, v, sink=None):
    B, T, H, d = q.shape
    KV = k.shape[2]
    G = H // KV
    nb = T // BLOCK
    scale = d ** -0.5
    qb = q.reshape(B, nb, BLOCK, KV, G, d).transpose(1, 0, 2, 3, 4, 5)
    sink_f = None if sink is None else sink.astype(jnp.float32).reshape(1, KV, G, 1, 1)

    def one_block(q_blk):
        s = jnp.einsum('bqkgd,bskd->bkgqs', q_blk, k, preferred_element_type=jnp.float32) * scale
        p = jax.nn.softmax(s, axis=-1) if sink_f is None else softmax_with_sink(s, sink_f)
        return jnp.einsum('bkgqs,bskd->bqkgd', p.astype(v.dtype), v)

    o = lax.map(one_block, qb)
    return o.transpose(1, 0, 2, 3, 4, 5).reshape(B, T, H * d)


def windowed_gqa_with_ctx(q, k, v, k_ctx, v_ctx, sink):
    B, T, H, d = q.shape
    KV = k.shape[2]
    G = H // KV
    P = k_ctx.shape[1]
    nb = T // BLOCK
    scale = d ** -0.5
    pad = ((0, 0), (BLOCK, BLOCK), (0, 0), (0, 0))
    kp = jnp.pad(k, pad).reshape(B, nb + 2, BLOCK, KV, d)
    vp = jnp.pad(v, pad).reshape(B, nb + 2, BLOCK, KV, d)
    kb = jnp.concatenate([kp[:, :nb], kp[:, 1:nb + 1], kp[:, 2:]], axis=2)
    vb = jnp.concatenate([vp[:, :nb], vp[:, 1:nb + 1], vp[:, 2:]], axis=2)
    qb = q.reshape(B, nb, BLOCK, KV, G, d)
    s_loc = jnp.einsum('bnqkgd,bnskd->bnkgqs', qb, kb, preferred_element_type=jnp.float32) * scale
    blk = jnp.arange(nb)[:, None] * BLOCK
    qpos = blk + jnp.arange(BLOCK)[None, :]
    kpos = blk - BLOCK + jnp.arange(3 * BLOCK)[None, :]
    diff = kpos[:, None, :] - qpos[:, :, None]
    valid = (jnp.abs(diff) <= WINDOW) & (kpos[:, None, :] >= 0) & (kpos[:, None, :] < T)
    s_loc = jnp.where(valid[None, :, None, None, :, :], s_loc, NEG_INF)
    s_ctx = jnp.einsum('bnqkgd,bskd->bnkgqs', qb, k_ctx, preferred_element_type=jnp.float32) * scale
    s = jnp.concatenate([s_ctx, s_loc], axis=-1)
    p = softmax_with_sink(s, sink.astype(jnp.float32).reshape(1, 1, KV, G, 1, 1)).astype(v.dtype)
    o = (jnp.einsum('bnkgqs,bskd->bnqkgd', p[..., :P], v_ctx)
         + jnp.einsum('bnkgqs,bnskd->bnqkgd', p[..., P:], vb))
    return o.reshape(B, T, H * d)


def expert_choice_moe(h, w_router, w_gate, w_up, w_down):
    B, T, D = h.shape
    cap = CAPACITY_FACTOR * T // N_EXPERTS
    logits = jnp.einsum('btd,de->bte', h, w_router, preferred_element_type=jnp.float32)
    aff = jax.nn.softmax(logits, axis=-1)
    g, idx = lax.top_k(jnp.swapaxes(aff, 1, 2), cap)
    xg = jax.vmap(lambda hb, ib: hb[ib])(h, idx)
    a = jnp.einsum('becd,edf->becf', xg, w_gate)
    u = jnp.einsum('becd,edf->becf', xg, w_up)
    y = jnp.einsum('becf,efd->becd', jax.nn.silu(a) * u, w_down) * g[..., None].astype(h.dtype)
    return jax.vmap(lambda yb, ib: jnp.zeros((T, D), yb.dtype).at[ib.reshape(-1)].add(yb.reshape(-1, D)))(y, idx)


def ffn_sublayer(x, shift, scale, gate, g_pre_ffn, w_router, w_gate, w_up, w_down, g_post_ffn):
    h = modulate(x, g_pre_ffn, shift, scale)
    return x + gate * rms_norm(expert_choice_moe(h, w_router, w_gate, w_up, w_down), g_post_ffn)


def context_layer(x, c_ctx, w_ada, b_ada, g_pre_attn, w_in, g_qa, g_ka, sink_b, w_out,
                  g_post_attn, g_pre_ffn, w_router, w_gate, w_up, w_down, g_post_ffn):
    sh1, sc1, gt1, sh2, sc2, gt2 = modulation(c_ctx[None, None, :], w_ada, b_ada)
    h = modulate(x, g_pre_attn, sh1, sc1)
    qa, ka, va, qb, kb, vb = attn_projections(h, w_in, g_qa, g_ka)
    oa = blocked_gqa(qa, ka, va)
    ob = blocked_gqa(qb, kb, vb, sink_b)
    o = jnp.einsum('btk,kd->btd', jnp.concatenate([oa, ob], axis=-1), w_out)
    x = x + gt1 * rms_norm(o, g_post_attn)
    x = ffn_sublayer(x, sh2, sc2, gt2, g_pre_ffn, w_router, w_gate, w_up, w_down, g_post_ffn)
    return x, ka, va, kb, vb


def latent_layer(x, c, k_a_ctx, v_a_ctx, k_b_ctx, v_b_ctx, rows, cols, w_ada, b_ada, g_pre_attn,
                 w_in, g_qa, g_ka, sink_b, w_out, g_post_attn, g_pre_ffn, w_router, w_gate,
                 w_up, w_down, g_post_ffn):
    sh1, sc1, gt1, sh2, sc2, gt2 = modulation(c[:, None, :], w_ada, b_ada)
    h = modulate(x, g_pre_attn, sh1, sc1)
    qa, ka, va, qb, kb, vb = attn_projections(h, w_in, g_qa, g_ka)
    qa, ka = rope_2d(qa, rows, cols), rope_2d(ka, rows, cols)
    qb, kb = rope_2d(qb, rows, cols), rope_2d(kb, rows, cols)
    oa = blocked_gqa(qa, jnp.concatenate([k_a_ctx, ka], axis=1), jnp.concatenate([v_a_ctx, va], axis=1))
    ob = windowed_gqa_with_ctx(qb, kb, vb, k_b_ctx, v_b_ctx, sink_b)
    o = jnp.einsum('btk,kd->btd', jnp.concatenate([oa, ob], axis=-1), w_out)
    x = x + gt1 * rms_norm(o, g_post_attn)
    return ffn_sublayer(x, sh2, sc2, gt2, g_pre_ffn, w_router, w_gate, w_up, w_down, g_post_ffn)


def setup_inputs(seed: int = 0) -> dict:
    key = jax.random.key(seed)
    ks = jax.random.split(key, 24)
    f32 = jnp.float32

    def nrm(k, shape, scale=1.0):
        return jax.random.normal(k, shape, f32) * scale

    def gain(k, shape):
        return 1.0 + 0.05 * jax.random.normal(k, shape, f32)

    return {
        'x_prompt': nrm(ks[0], (BATCH, SEQ, D_MODEL)),
        'x_sample': nrm(ks[1], (DEC_BATCH, DEC_SEQ, D_MODEL)),
        'cache_a_k': nrm(ks[2], (DEC_BATCH, DEPTH, PAST_LEN, N_KV_A, HEAD_DIM)),
        'cache_a_v': nrm(ks[3], (DEC_BATCH, DEPTH, PAST_LEN, N_KV_A, HEAD_DIM)),
        'cache_b_k': nrm(ks[4], (DEC_BATCH, DEPTH, PAST_LEN, N_KV_B, HEAD_DIM)),
        'cache_b_v': nrm(ks[5], (DEC_BATCH, DEPTH, PAST_LEN, N_KV_B, HEAD_DIM)),
        'c': nrm(ks[6], (DEC_BATCH, D_MODEL)),
        'c_ctx': nrm(ks[7], (D_MODEL,)),
        'w_ada': nrm(ks[8], (DEPTH, D_MODEL, N_MOD * D_MODEL), 0.5 * D_MODEL ** -0.5),
        'b_ada': nrm(ks[9], (DEPTH, N_MOD * D_MODEL), 0.02),
        'g_pre_attn': gain(ks[10], (DEPTH, D_MODEL)),
        'w_in': nrm(ks[11], (DEPTH, D_MODEL, IN_W), D_MODEL ** -0.5),
        'g_qa': gain(ks[12], (DEPTH, HEAD_DIM)),
        'g_ka': gain(ks[13], (DEPTH, HEAD_DIM)),
        'sink_b': nrm(ks[14], (DEPTH, N_HEADS_B), 0.5),
        'w_out': nrm(ks[15], (DEPTH, MIX_W, D_MODEL), MIX_W ** -0.5),
        'g_post_attn': gain(ks[16], (DEPTH, D_MODEL)),
        'g_pre_ffn': gain(ks[17], (DEPTH, D_MODEL)),
        'w_router': nrm(ks[18], (DEPTH, D_MODEL, N_EXPERTS), D_MODEL ** -0.5),
        'w_gate': nrm(ks[19], (DEPTH, N_EXPERTS, D_MODEL, D_FF_EXPERT), D_MODEL ** -0.5),
        'w_up': nrm(ks[20], (DEPTH, N_EXPERTS, D_MODEL, D_FF_EXPERT), D_MODEL ** -0.5),
        'w_down': nrm(ks[21], (DEPTH, N_EXPERTS, D_FF_EXPERT, D_MODEL), D_FF_EXPERT ** -0.5),
        'g_post_ffn': gain(ks[22], (DEPTH, D_MODEL)),
    }


def reference(x_prompt, x_sample, cache_a_k, cache_a_v, cache_b_k, cache_b_v, c, c_ctx,
              w_ada, b_ada, g_pre_attn, w_in, g_qa, g_ka, sink_b, w_out, g_post_attn,
              g_pre_ffn, w_router, w_gate, w_up, w_down, g_post_ffn):
    rows, cols = grid_positions(x_sample.shape[1])
    y_prompt = x_prompt
    y_sample = x_sample
    ak_list, av_list, bk_list, bv_list = [], [], [], []
    for l in range(DEPTH):
        y_prompt, ka, va, kb, vb = context_layer(
            y_prompt, c_ctx, w_ada[l], b_ada[l], g_pre_attn[l], w_in[l], g_qa[l], g_ka[l],
            sink_b[l], w_out[l], g_post_attn[l], g_pre_ffn[l], w_router[l], w_gate[l],
            w_up[l], w_down[l], g_post_ffn[l])
        ak_list.append(ka)
        av_list.append(va)
        bk_list.append(kb)
        bv_list.append(vb)
        y_sample = latent_layer(
            y_sample, c, cache_a_k[:, l], cache_a_v[:, l], cache_b_k[:, l], cache_b_v[:, l],
            rows, cols, w_ada[l], b_ada[l], g_pre_attn[l], w_in[l], g_qa[l], g_ka[l],
            sink_b[l], w_out[l], g_post_attn[l], g_pre_ffn[l], w_router[l], w_gate[l],
            w_up[l], w_down[l], g_post_ffn[l])
    new_a_k = jnp.stack(ak_list, axis=1)
    new_a_v = jnp.stack(av_list, axis=1)
    new_b_k = jnp.stack(bk_list, axis=1)
    new_b_v = jnp.stack(bv_list, axis=1)
    return (y_prompt, y_sample, new_a_k, new_a_v, new_b_k, new_b_v)
```

```python
import functools

import jax
import jax.numpy as jnp
from jax import lax
from jax.experimental import pallas as pl
from jax.experimental.pallas import tpu as pltpu

F32 = jnp.float32
BF16 = jnp.bfloat16

D_MODEL = 1024
HEAD_DIM = 64
N_HEADS = 8
N_KV = 2
Q_W = N_HEADS * HEAD_DIM
KV_W = N_KV * HEAD_DIM
IN_W = 2 * (Q_W + 2 * KV_W)
GRID_W = 64
BLOCK = 128
WINDOW = 128
ROPE_THETA = 10000.0
N_EXPERTS = 16
CAPACITY_FACTOR = 2
N_MOD = 6
EPS = 1e-6
NEG_INF = -1e30
LANES = 128
MOD_ROWS = 16
VMEM_LIMIT = 56 * 1024 * 1024

EXP_STEPS = (64, 32, 16, 8, 4, 2, 1)
MANTISSA_STEPS = 36


def _dot(a, b):
    return jnp.dot(a, b, preferred_element_type=F32)


def _dot_nt(a, b):
    return lax.dot_general(a, b, (((1,), (1,)), ((), ())), preferred_element_type=F32)


def _dot_tn(a, b):
    return lax.dot_general(a, b, (((0,), (0,)), ((), ())), preferred_element_type=F32)


def _split_bf16(x):
    hi = x.astype(BF16)
    lo = (x - hi.astype(F32)).astype(BF16)
    return hi, lo


def _rms(x):
    return x * lax.rsqrt(jnp.mean(x * x, axis=-1, keepdims=True) + EPS)


def _silu(x):
    return x * (1.0 / (1.0 + jnp.exp(-x)))


def _params(*sem):
    return pltpu.CompilerParams(dimension_semantics=sem, vmem_limit_bytes=VMEM_LIMIT)


def _mod_kernel(cond_ref, w_ref, b_ref, out_ref):
    a_hi, a_lo = _split_bf16(_silu(cond_ref[...]))
    w_hi, w_lo = _split_bf16(w_ref[...])
    out_ref[...] = _dot(a_hi, w_hi) + _dot(a_hi, w_lo) + _dot(a_lo, w_hi) + b_ref[...]


def _modulation(cond, w_ada, b_ada):
    n = w_ada.shape[1]
    tn = 1536
    return pl.pallas_call(
        _mod_kernel,
        grid=(n // tn,),
        in_specs=[
            pl.BlockSpec((MOD_ROWS, D_MODEL), lambda j: (0, 0)),
            pl.BlockSpec((D_MODEL, tn), lambda j: (0, j)),
            pl.BlockSpec((1, tn), lambda j: (0, j)),
        ],
        out_specs=pl.BlockSpec((MOD_ROWS, tn), lambda j: (0, j)),
        out_shape=jax.ShapeDtypeStruct((MOD_ROWS, n), F32),
        compiler_params=_params("arbitrary"),
        name="modulation",
    )(cond, w_ada, b_ada)


def _rotate_pairs(x):
    lane = lax.broadcasted_iota(jnp.int32, x.shape, 1)
    return jnp.where((lane & 31) < 16, pltpu.roll(x, LANES - 16, 1), pltpu.roll(x, 16, 1))


def _proj_kernel(x_ref, sh_ref, sc_ref, gpre_ref, w_ref, gq_ref, gk_ref, bd_ref, *rest, rope):
    if rope:
        cos_ref, sin_ref, q_ref, kv_ref = rest
    else:
        q_ref, kv_ref, ka_ref, va_ref, kb_ref, vb_ref = rest
    h = _rms(x_ref[...]) * gpre_ref[...]
    h = h * (1.0 + sc_ref[0]) + sh_ref[0]
    p = _dot(h.astype(BF16), w_ref[...])
    bd = bd_ref[...]

    def head_norm(t, g):
        ss = _dot((t * t).astype(BF16), bd[: t.shape[1], : t.shape[1]])
        return t * lax.rsqrt(ss * (1.0 / HEAD_DIM) + EPS) * g

    gq = gq_ref[...]
    qa = [head_norm(p[:, c:c + 2 * LANES], gq[:, c:c + 2 * LANES]) for c in (0, 2 * LANES)]
    qa = [t[:, c:c + LANES] for t in qa for c in (0, LANES)]
    qb = [p[:, Q_W + c:Q_W + c + LANES] for c in range(0, Q_W, LANES)]
    k0 = 2 * Q_W
    ka = head_norm(p[:, k0:k0 + KV_W], gk_ref[...])
    va = p[:, k0 + KV_W:k0 + 2 * KV_W]
    kb = p[:, k0 + 2 * KV_W:k0 + 3 * KV_W]
    vb = p[:, k0 + 3 * KV_W:k0 + 4 * KV_W]
    if rope:
        cos = cos_ref[...]
        sin = sin_ref[...]
        rot = lambda t: t * cos + _rotate_pairs(t) * sin
        ka_r, kb_r = rot(ka), rot(kb)
        qa = [rot(t) for t in qa]
        qb = [rot(t) for t in qb]
    else:
        ka_r, kb_r = ka, kb
        ka_ref[...] = ka
        va_ref[...] = va
        kb_ref[...] = kb
        vb_ref[...] = vb
    scale = HEAD_DIM ** -0.5
    q_ref[...] = jnp.concatenate([t * scale for t in qa + qb], axis=1).astype(BF16)
    kv_ref[...] = jnp.concatenate([ka_r, va, kb_r, vb], axis=1).astype(BF16)


def _projection(x2d, mod3, mod_row, rows_per_batch, tm, w_in_p, gpre, gq, gk, bd, rope_tabs):
    n_rows = x2d.shape[0]
    tiles_per_batch = rows_per_batch // tm
    rope = rope_tabs is not None

    def mod_spec(j):
        return pl.BlockSpec(
            (1, 1, D_MODEL), lambda i: (mod_row(i // tiles_per_batch) * N_MOD + j, 0, 0))

    const = lambda shape: pl.BlockSpec(shape, lambda i: (0, 0))
    in_specs = [
        pl.BlockSpec((tm, D_MODEL), lambda i: (i, 0)),
        mod_spec(0), mod_spec(1),
        const((1, D_MODEL)), const((D_MODEL, IN_W)), const((1, Q_W)), const((1, KV_W)),
        const((2 * LANES, 2 * LANES)),
    ]
    args = [x2d, mod3, mod3, gpre, w_in_p, gq, gk, bd]
    out_specs = [pl.BlockSpec((tm, 2 * Q_W), lambda i: (i, 0)),
                 pl.BlockSpec((tm, 4 * KV_W), lambda i: (i, 0))]
    out_shape = [jax.ShapeDtypeStruct((n_rows, 2 * Q_W), BF16),
                 jax.ShapeDtypeStruct((n_rows, 4 * KV_W), BF16)]
    if rope:
        tab_spec = pl.BlockSpec((tm, LANES), lambda i: (i % tiles_per_batch, 0))
        in_specs += [tab_spec, tab_spec]
        args += list(rope_tabs)
    else:
        out_specs += [pl.BlockSpec((tm, KV_W), lambda i: (i, 0))] * 4
        out_shape += [jax.ShapeDtypeStruct((n_rows, KV_W), F32)] * 4
    return pl.pallas_call(
        functools.partial(_proj_kernel, rope=rope),
        grid=(n_rows // tm,),
        in_specs=in_specs,
        out_specs=out_specs,
        out_shape=out_shape,
        compiler_params=_params("parallel"),
        name="projection_latent" if rope else "projection_context",
    )(*args)


def _stack_heads(q):
    lane = lax.broadcasted_iota(jnp.int32, (q.shape[0], LANES), 1)
    blocks = []
    for j in range(Q_W // LANES):
        chunk = q[:, j * LANES:(j + 1) * LANES]
        blocks.append(jnp.where(lane < HEAD_DIM, chunk, jnp.zeros_like(chunk)))
        blocks.append(jnp.where(lane >= HEAD_DIM, chunk, jnp.zeros_like(chunk)))
    return jnp.concatenate(blocks, axis=0)


def _unstack_heads(o, tq):
    lane = lax.broadcasted_iota(jnp.int32, (tq, LANES), 1)
    chunks = []
    for j in range(Q_W // LANES):
        lo = o[(2 * j) * tq:(2 * j + 1) * tq]
        hi = o[(2 * j + 1) * tq:(2 * j + 2) * tq]
        chunks.append(jnp.where(lane < HEAD_DIM, lo, hi))
    return jnp.concatenate(chunks, axis=1)


def _attend(qm, keys, values, masks, sink_col):
    scores = [_dot_nt(qm, k) for k in keys]
    scores = [s if m is None else jnp.where(m, s, NEG_INF) for s, m in zip(scores, masks)]
    mx = functools.reduce(jnp.maximum, [jnp.max(s, axis=-1, keepdims=True) for s in scores])
    if sink_col is not None:
        mx = jnp.maximum(mx, sink_col)
    es = [jnp.exp(s - mx) for s in scores]
    denom = functools.reduce(jnp.add, [jnp.sum(e, axis=-1, keepdims=True) for e in es])
    if sink_col is not None:
        denom = denom + jnp.exp(sink_col - mx)
    o = functools.reduce(jnp.add, [_dot(e.astype(BF16), v) for e, v in zip(es, values)])
    return o * (1.0 / denom)


def _attn_kernel(sink_ref, q_ref, kv_ref, x_ref, gt1_ref, sh2_ref, sc2_ref, gpost_ref, gpre2_ref,
                 wout_ref, wr_hi_ref, wr_lo_ref, *rest, latent, tq):
    if latent:
        cak_ref, cav_ref, cbk_ref, cbv_ref, x1_ref, h2_ref, afft_ref = rest
    else:
        x1_ref, h2_ref, afft_ref = rest
    q = q_ref[...]
    m_rows = N_HEADS * tq
    sink_col = jnp.concatenate(
        [jnp.full((tq, 1), sink_ref[(i // 2) + 4 * (i % 2)], F32) for i in range(N_HEADS)], axis=0)
    qma = _stack_heads(q[:, :Q_W])
    qmb = _stack_heads(q[:, Q_W:])
    if latent:
        t0 = pl.program_id(1) * tq
        ka = kv_ref[:, 0:KV_W]
        va = kv_ref[:, KV_W:2 * KV_W]
        oa = _attend(qma, [cak_ref[0].astype(BF16), ka], [cav_ref[0].astype(BF16), va],
                     [None, None], None)
        n_tok = kv_ref.shape[0]
        band = 3 * BLOCK
        start = pl.multiple_of(jnp.clip(t0 - BLOCK, 0, n_tok - band), BLOCK)
        kb = kv_ref[pl.ds(start, band), 2 * KV_W:3 * KV_W]
        vb = kv_ref[pl.ds(start, band), 3 * KV_W:4 * KV_W]
        qpos = t0 + (lax.broadcasted_iota(jnp.int32, (m_rows, band), 0) & (tq - 1))
        kpos = start + lax.broadcasted_iota(jnp.int32, (m_rows, band), 1)
        valid = jnp.abs(kpos - qpos) <= WINDOW
        ob = _attend(qmb, [cbk_ref[0].astype(BF16), kb], [cbv_ref[0].astype(BF16), vb],
                     [None, valid], sink_col)
    else:
        oa = _attend(qma, [kv_ref[:, 0:KV_W]], [kv_ref[:, KV_W:2 * KV_W]], [None], None)
        ob = _attend(qmb, [kv_ref[:, 2 * KV_W:3 * KV_W]], [kv_ref[:, 3 * KV_W:4 * KV_W]],
                     [None], sink_col)
    o = jnp.concatenate([_unstack_heads(oa, tq), _unstack_heads(ob, tq)], axis=1)
    proj = _dot(o.astype(BF16), wout_ref[...])
    x1 = x_ref[...] + gt1_ref[0] * (_rms(proj) * gpost_ref[...])
    x1_ref[...] = x1
    h2 = (_rms(x1) * gpre2_ref[...]) * (1.0 + sc2_ref[0]) + sh2_ref[0]
    h2_ref[...] = h2.astype(BF16)
    h_hi, h_lo = _split_bf16(h2)
    wr_hi = wr_hi_ref[...]
    logits = _dot(h_hi, wr_hi) + _dot(h_lo, wr_hi) + _dot(h_hi, wr_lo_ref[...])
    lane = lax.broadcasted_iota(jnp.int32, logits.shape, 1)
    logits = jnp.where(lane < N_EXPERTS, logits, NEG_INF)
    ex = jnp.exp(logits - jnp.max(logits, axis=-1, keepdims=True))
    aff = ex * (1.0 / jnp.sum(ex, axis=-1, keepdims=True))
    afft_ref[0] = aff.T[:N_EXPERTS]


def _attention(sink_p, q, kv, x2d, mod3, mod_row, n_batch, n_tok, tq, gpost, gpre2, wout_p,
               wr_hi, wr_lo, caches):
    latent = caches is not None
    nq = n_tok // tq
    row = lambda b, i: (b * nq + i, 0)

    def mod_spec(j):
        return pl.BlockSpec((1, 1, D_MODEL), lambda b, i: (mod_row(b) * N_MOD + j, 0, 0))

    const = lambda shape: pl.BlockSpec(shape, lambda b, i: (0, 0))
    in_specs = [
        pl.BlockSpec(memory_space=pltpu.SMEM),
        pl.BlockSpec((tq, 2 * Q_W), row),
        pl.BlockSpec((n_tok, 4 * KV_W), lambda b, i: (b, 0)),
        pl.BlockSpec((tq, D_MODEL), row),
        mod_spec(2), mod_spec(3), mod_spec(4),
        const((1, D_MODEL)), const((1, D_MODEL)),
        const((2 * Q_W, D_MODEL)), const((D_MODEL, LANES)), const((D_MODEL, LANES)),
    ]
    args = [sink_p, q, kv, x2d, mod3, mod3, mod3, gpost, gpre2, wout_p, wr_hi, wr_lo]
    if latent:
        cache_spec = pl.BlockSpec((1,) + caches[0].shape[1:], lambda b, i: (b, 0, 0))
        in_specs += [cache_spec] * 4
        args += list(caches)
    out_specs = [
        pl.BlockSpec((tq, D_MODEL), row),
        pl.BlockSpec((tq, D_MODEL), row),
        pl.BlockSpec((1, N_EXPERTS, tq), lambda b, i: (b, 0, i)),
    ]
    out_shape = [
        jax.ShapeDtypeStruct((n_batch * n_tok, D_MODEL), F32),
        jax.ShapeDtypeStruct((n_batch * n_tok, D_MODEL), BF16),
        jax.ShapeDtypeStruct((n_batch, N_EXPERTS, n_tok), F32),
    ]
    return pl.pallas_call(
        functools.partial(_attn_kernel, latent=latent, tq=tq),
        grid=(n_batch, nq),
        in_specs=in_specs,
        out_specs=out_specs,
        out_shape=out_shape,
        compiler_params=_params("parallel", "parallel"),
        name="attention_latent" if latent else "attention_context",
    )(*args)


def _route_kernel(afft_ref, tri_ref, slot_ref, *, cap):
    a = afft_ref[...]
    capf = float(cap)
    count_ge = lambda t: jnp.sum(jnp.where(a >= t, 1.0, 0.0), axis=1, keepdims=True)
    p = jnp.full((a.shape[0], 1), 2.0, F32)
    for step in EXP_STEPS:
        cand = p * (2.0 ** -step)
        p = jnp.where(count_ge(cand) < capf, cand, p)
    lo0 = p * 0.5

    def refine(_, carry):
        lo, delta = carry
        cand = lo + delta
        return jnp.where(count_ge(cand) >= capf, cand, lo), delta * 0.5

    thr, _ = lax.fori_loop(0, MANTISSA_STEPS, refine, (lo0, lo0 * 0.5))
    above = a > thr
    tied = a == thr
    need = capf - jnp.sum(jnp.where(above, 1.0, 0.0), axis=1, keepdims=True)
    tri = tri_ref[...]
    tie_rank = _dot(jnp.where(tied, 1.0, 0.0).astype(BF16), tri)
    sel = jnp.logical_or(above, jnp.logical_and(tied, tie_rank < need))
    slot = _dot(jnp.where(sel, 1.0, 0.0).astype(BF16), tri)
    slot_ref[...] = jnp.where(sel, slot, -1.0).astype(jnp.int32)


def _route(afft2d, tri, cap):
    rows, n_tok = afft2d.shape
    return pl.pallas_call(
        functools.partial(_route_kernel, cap=cap),
        grid=(1,),
        in_specs=[pl.BlockSpec((rows, n_tok), lambda i: (0, 0)),
                  pl.BlockSpec((n_tok, n_tok), lambda i: (0, 0))],
        out_specs=pl.BlockSpec((rows, n_tok), lambda i: (0, 0)),
        out_shape=jax.ShapeDtypeStruct((rows, n_tok), jnp.int32),
        compiler_params=_params("arbitrary"),
        name="route",
    )(afft2d, tri)


def _one_hot_rows(slot_row, cap):
    return slot_row == lax.broadcasted_iota(jnp.int32, (cap, slot_row.shape[1]), 0)


def _gather_kernel(slot_ref, afft_ref, h_ref, xg_ref, g_ref, *, cap):
    h = h_ref[...]
    for e in range(N_EXPERTS):
        onehot = _one_hot_rows(slot_ref[e:e + 1, :], cap)
        xg_ref[e] = _dot(jnp.where(onehot, 1.0, 0.0).astype(BF16), h).astype(BF16)
        g = jnp.sum(jnp.where(onehot, afft_ref[0, e:e + 1, :], 0.0), axis=1, keepdims=True)
        g_ref[e] = jnp.broadcast_to(g, (cap, LANES))


def _gather(slot2d, afft, h2, n_batch, n_tok, cap):
    return pl.pallas_call(
        functools.partial(_gather_kernel, cap=cap),
        grid=(n_batch,),
        in_specs=[
            pl.BlockSpec((N_EXPERTS, n_tok), lambda b: (b, 0)),
            pl.BlockSpec((1, N_EXPERTS, n_tok), lambda b: (b, 0, 0)),
            pl.BlockSpec((n_tok, D_MODEL), lambda b: (b, 0)),
        ],
        out_specs=[
            pl.BlockSpec((N_EXPERTS, cap, D_MODEL), lambda b: (0, b, 0)),
            pl.BlockSpec((N_EXPERTS, cap, LANES), lambda b: (0, b, 0)),
        ],
        out_shape=[
            jax.ShapeDtypeStruct((N_EXPERTS, n_batch * cap, D_MODEL), BF16),
            jax.ShapeDtypeStruct((N_EXPERTS, n_batch * cap, LANES), F32),
        ],
        compiler_params=_params("parallel"),
        name="gather",
    )(slot2d, afft, h2)


def _ffn_kernel(xc_ref, xl_ref, gc_ref, gl_ref, wg_ref, wu_ref, wd_ref, yc_ref, yl_ref,
                wg_bf, wu_bf, wd_bf):
    j = pl.program_id(1)

    @pl.when(j == 0)
    def _():
        wg_bf[...] = wg_ref[0].astype(BF16)
        wu_bf[...] = wu_ref[0].astype(BF16)
        wd_bf[...] = wd_ref[0].astype(BF16)

    def run(x_ref, g_ref, y_ref):
        x = x_ref[0]
        act = (_silu(_dot(x, wg_bf[...])) * _dot(x, wu_bf[...])).astype(BF16)
        y = _dot(act, wd_bf[...])
        y_ref[0] = (y * jnp.tile(g_ref[0], (1, D_MODEL // LANES))).astype(BF16)

    @pl.when(j == 0)
    def _():
        run(xc_ref, gc_ref, yc_ref)

    @pl.when(j > 0)
    def _():
        run(xl_ref, gl_ref, yl_ref)


def _ffn(xc, xl, gc, gl, w_gate, w_up, w_down):
    tm = xc.shape[1]
    n_lat = xl.shape[1] // tm
    d_ff = w_gate.shape[2]
    ctx_map = lambda e, j: (e, 0, 0)
    lat_map = lambda e, j: (e, jnp.maximum(j - 1, 0), 0)
    w_map = lambda e, j: (e, 0, 0)
    return pl.pallas_call(
        _ffn_kernel,
        grid=(N_EXPERTS, 1 + n_lat),
        in_specs=[
            pl.BlockSpec((1, tm, D_MODEL), ctx_map),
            pl.BlockSpec((1, tm, D_MODEL), lat_map),
            pl.BlockSpec((1, tm, LANES), ctx_map),
            pl.BlockSpec((1, tm, LANES), lat_map),
            pl.BlockSpec((1, D_MODEL, d_ff), w_map),
            pl.BlockSpec((1, D_MODEL, d_ff), w_map),
            pl.BlockSpec((1, d_ff, D_MODEL), w_map),
        ],
        out_specs=[
            pl.BlockSpec((1, tm, D_MODEL), ctx_map),
            pl.BlockSpec((1, tm, D_MODEL), lat_map),
        ],
        out_shape=[jax.ShapeDtypeStruct(xc.shape, BF16), jax.ShapeDtypeStruct(xl.shape, BF16)],
        scratch_shapes=[pltpu.VMEM((D_MODEL, d_ff), BF16), pltpu.VMEM((D_MODEL, d_ff), BF16),
                        pltpu.VMEM((d_ff, D_MODEL), BF16)],
        compiler_params=_params("arbitrary", "arbitrary"),
        name="expert_ffn",
    )(xc, xl, gc, gl, w_gate, w_up, w_down)


def _scatter_kernel(slot_ref, y_ref, x1_ref, gt2_ref, gpost_ref, out_ref, *, cap):
    onehot = jnp.concatenate(
        [jnp.where(_one_hot_rows(slot_ref[e:e + 1, :], cap), 1.0, 0.0).astype(BF16)
         for e in range(N_EXPERTS)], axis=0)
    y = y_ref[...].reshape(N_EXPERTS * cap, D_MODEL)
    moe = _dot_tn(onehot, y)
    out_ref[...] = x1_ref[...] + gt2_ref[0] * (_rms(moe) * gpost_ref[...])


def _scatter(slot2d, y, x1, mod3, mod_row, n_batch, n_tok, cap, gpost):
    return pl.pallas_call(
        functools.partial(_scatter_kernel, cap=cap),
        grid=(n_batch,),
        in_specs=[
            pl.BlockSpec((N_EXPERTS, n_tok), lambda b: (b, 0)),
            pl.BlockSpec((N_EXPERTS, cap, D_MODEL), lambda b: (0, b, 0)),
            pl.BlockSpec((n_tok, D_MODEL), lambda b: (b, 0)),
            pl.BlockSpec((1, 1, D_MODEL), lambda b: (mod_row(b) * N_MOD + 5, 0, 0)),
            pl.BlockSpec((1, D_MODEL), lambda b: (0, 0)),
        ],
        out_specs=pl.BlockSpec((n_tok, D_MODEL), lambda b: (b, 0)),
        out_shape=jax.ShapeDtypeStruct((n_batch * n_tok, D_MODEL), F32),
        compiler_params=_params("parallel"),
        name="scatter",
    )(slot2d, y, x1, mod3, gpost)


def _pair_heads(w, axis):
    shape = w.shape
    w = w.reshape(shape[:axis] + (2, 4, HEAD_DIM) + shape[axis + 1:])
    w = jnp.swapaxes(w, axis, axis + 1)
    return w.reshape(shape)


def _rope_tables(n_tok):
    half = HEAD_DIM // 4
    freqs = ROPE_THETA ** (-jnp.arange(half, dtype=F32) / half)
    rows = jnp.repeat(jnp.arange(n_tok // GRID_W, dtype=F32), GRID_W)
    cols = jnp.tile(jnp.arange(GRID_W, dtype=F32), n_tok // GRID_W)
    d = jnp.arange(HEAD_DIM)
    pos = jnp.where((d // (2 * half))[None, :] == 0, rows[:, None], cols[:, None])
    ang = pos * freqs[d % half][None, :]
    sign = jnp.where((d % (2 * half)) < half, -1.0, 1.0).astype(F32)
    cos = jnp.tile(jnp.cos(ang), (1, LANES // HEAD_DIM))
    sin = jnp.tile(jnp.sin(ang) * sign[None, :], (1, LANES // HEAD_DIM))
    return cos, sin


def _prefix_matrix(n):
    r = jnp.arange(n)
    return (r[:, None] < r[None, :]).astype(BF16)


def kernel(x_prompt, x_sample, cache_a_k, cache_a_v, cache_b_k, cache_b_v, c, c_ctx, w_ada, b_ada,
           g_pre_attn, w_in, g_qa, g_ka, sink_b, w_out, g_post_attn, g_pre_ffn, w_router, w_gate,
           w_up, w_down, g_post_ffn):
    n_ctx, t_ctx, _ = x_prompt.shape
    n_lat, t_lat, _ = x_sample.shape
    assert w_ada.shape[0] == 1, "single-layer step"
    assert n_lat + 1 <= MOD_ROWS

    w_in0 = w_in[0]
    c1, c2, c3, c4, c5 = Q_W, Q_W + KV_W, Q_W + 2 * KV_W, 2 * Q_W + 2 * KV_W, 2 * Q_W + 3 * KV_W
    w_in_p = jnp.concatenate(
        [_pair_heads(w_in0[:, :c1], 1), _pair_heads(w_in0[:, c3:c4], 1),
         w_in0[:, c1:c3], w_in0[:, c4:]], axis=1).astype(BF16)
    w_out0 = w_out[0]
    w_out_p = jnp.concatenate(
        [_pair_heads(w_out0[:Q_W], 0), _pair_heads(w_out0[Q_W:], 0)], axis=0).astype(BF16)
    wr = jnp.pad(w_router[0], ((0, 0), (0, LANES - N_EXPERTS)))
    wr_hi = wr.astype(BF16)
    wr_lo = (wr - wr_hi.astype(F32)).astype(BF16)
    gq = jnp.tile(g_qa[0], N_HEADS)[None, :]
    gk = jnp.tile(g_ka[0], N_KV)[None, :]
    blk = jnp.arange(2 * LANES) // HEAD_DIM
    bd = (blk[:, None] == blk[None, :]).astype(BF16)
    sink = sink_b[0].astype(F32)
    cond = jnp.zeros((MOD_ROWS, D_MODEL), F32).at[:n_lat].set(c).at[n_lat].set(c_ctx)

    mod = _modulation(cond, w_ada[0], b_ada[0][None, :])
    mod3 = mod.reshape(MOD_ROWS * N_MOD, 1, D_MODEL)

    xc = x_prompt.reshape(n_ctx * t_ctx, D_MODEL)
    xl = x_sample.reshape(n_lat * t_lat, D_MODEL)
    gpre1, gpost1 = g_pre_attn[0][None, :], g_post_attn[0][None, :]
    gpre2, gpost2 = g_pre_ffn[0][None, :], g_post_ffn[0][None, :]

    ctx_row = lambda b: n_lat
    lat_row = lambda b: b

    qc, kvc, ka, va, kb, vb = _projection(
        xc, mod3, ctx_row, t_ctx, t_ctx, w_in_p, gpre1, gq, gk, bd, None)
    x1c, h2c, afftc = _attention(
        sink, qc, kvc, xc, mod3, ctx_row, n_ctx, t_ctx, t_ctx, gpost1, gpre2, w_out_p,
        wr_hi, wr_lo, None)

    ql, kvl = _projection(
        xl, mod3, lat_row, t_lat, 256, w_in_p, gpre1, gq, gk, bd, _rope_tables(t_lat))
    caches = [t[:, 0].reshape(n_lat, t.shape[2], KV_W)
              for t in (cache_a_k, cache_a_v, cache_b_k, cache_b_v)]
    x1l, h2l, afftl = _attention(
        sink, ql, kvl, xl, mod3, lat_row, n_lat, t_lat, BLOCK, gpost1, gpre2, w_out_p,
        wr_hi, wr_lo, caches)

    cap_c = CAPACITY_FACTOR * t_ctx // N_EXPERTS
    cap_l = CAPACITY_FACTOR * t_lat // N_EXPERTS
    slot_c = _route(afftc.reshape(n_ctx * N_EXPERTS, t_ctx), _prefix_matrix(t_ctx), cap_c)
    slot_l = _route(afftl.reshape(n_lat * N_EXPERTS, t_lat), _prefix_matrix(t_lat), cap_l)
    xgc, gc = _gather(slot_c, afftc, h2c, n_ctx, t_ctx, cap_c)
    xgl, gl = _gather(slot_l, afftl, h2l, n_lat, t_lat, cap_l)
    assert xgl.shape[1] % xgc.shape[1] == 0
    yc, yl = _ffn(xgc, xgl, gc, gl, w_gate[0], w_up[0], w_down[0])
    y_prompt = _scatter(slot_c, yc, x1c, mod3, ctx_row, n_ctx, t_ctx, cap_c, gpost2)
    y_sample = _scatter(slot_l, yl, x1l, mod3, lat_row, n_lat, t_lat, cap_l, gpost2)

    kv_shape = (n_ctx, 1, t_ctx, N_KV, HEAD_DIM)
    return (y_prompt.reshape(x_prompt.shape), y_sample.reshape(x_sample.shape),
            ka.reshape(kv_shape), va.reshape(kv_shape), kb.reshape(kv_shape), vb.reshape(kv_shape))
```

```python
import functools

import jax
import jax.numpy as jnp
from jax import lax
from jax.experimental import pallas as pl
from jax.experimental.pallas import tpu as pltpu

F32 = jnp.float32
BF16 = jnp.bfloat16

D_MODEL = 1024
HEAD_DIM = 64
N_HEADS = 8
N_KV = 2
Q_W = N_HEADS * HEAD_DIM
KV_W = N_KV * HEAD_DIM
IN_W = 2 * (Q_W + 2 * KV_W)
GRID_W = 64
BLOCK = 128
WINDOW = 128
ROPE_THETA = 10000.0
N_EXPERTS = 16
CAPACITY_FACTOR = 2
N_MOD = 6
EPS = 1e-6
NEG_INF = -1e30
LOG2E = 1.4426950408889634
LANES = 128
MOD_ROWS = 16
VMEM_LIMIT = 56 * 1024 * 1024
FFN_HIDDEN_CHUNK = 256

EXP_STEPS = (64, 32, 16, 8, 4, 2, 1)
MANTISSA_STEPS = 36


def _dot(a, b):
    return jnp.dot(a, b, preferred_element_type=F32)


def _dot_nt(a, b):
    return lax.dot_general(a, b, (((1,), (1,)), ((), ())), preferred_element_type=F32)


def _dot_tn(a, b):
    return lax.dot_general(a, b, (((0,), (0,)), ((), ())), preferred_element_type=F32)


def _split_bf16(x):
    hi = x.astype(BF16)
    lo = (x - hi.astype(F32)).astype(BF16)
    return hi, lo


def _rms(x):
    return x * lax.rsqrt(jnp.mean(x * x, axis=-1, keepdims=True) + EPS)


def _silu(x):
    return x * (1.0 / (1.0 + jnp.exp(-x)))


def _params(*sem):
    return pltpu.CompilerParams(dimension_semantics=sem, vmem_limit_bytes=VMEM_LIMIT)


def _mod_kernel(cond_ref, w_ref, b_ref, out_ref):
    a_hi, a_lo = _split_bf16(_silu(cond_ref[...]))
    w_hi, w_lo = _split_bf16(w_ref[...])
    out_ref[...] = _dot(a_hi, w_hi) + _dot(a_hi, w_lo) + _dot(a_lo, w_hi) + b_ref[...]


def _modulation(cond, w_ada, b_ada):
    n = w_ada.shape[1]
    tn = 1536
    return pl.pallas_call(
        _mod_kernel,
        grid=(n // tn,),
        in_specs=[
            pl.BlockSpec((MOD_ROWS, D_MODEL), lambda j: (0, 0)),
            pl.BlockSpec((D_MODEL, tn), lambda j: (0, j)),
            pl.BlockSpec((1, tn), lambda j: (0, j)),
        ],
        out_specs=pl.BlockSpec((MOD_ROWS, tn), lambda j: (0, j)),
        out_shape=jax.ShapeDtypeStruct((MOD_ROWS, n), F32),
        compiler_params=_params("arbitrary"),
        name="modulation",
    )(cond, w_ada, b_ada)


def _rotate_pairs(x):
    lane = lax.broadcasted_iota(jnp.int32, x.shape, 1)
    return jnp.where((lane & 31) < 16, pltpu.roll(x, LANES - 16, 1), pltpu.roll(x, 16, 1))


def _proj_kernel(x_ref, sh_ref, sc_ref, gpre_ref, w_ref, gq_ref, gk_ref, bd_ref, *rest, rope):
    if rope:
        cos_ref, sin_ref, q_ref, k_ref, vt_ref = rest
    else:
        q_ref, k_ref, vt_ref, ka_ref, va_ref, kb_ref, vb_ref = rest
    h = _rms(x_ref[...]) * gpre_ref[...]
    h = h * (1.0 + sc_ref[0]) + sh_ref[0]
    p = _dot(h.astype(BF16), w_ref[...])
    bd = bd_ref[...]

    def head_norm(t, g):
        ss = _dot((t * t).astype(BF16), bd[: t.shape[1], : t.shape[1]])
        return t * lax.rsqrt(ss * (1.0 / HEAD_DIM) + EPS) * g

    gq = gq_ref[...]
    qa = [head_norm(p[:, c:c + 2 * LANES], gq[:, c:c + 2 * LANES]) for c in (0, 2 * LANES)]
    qa = [t[:, c:c + LANES] for t in qa for c in (0, LANES)]
    qb = [p[:, Q_W + c:Q_W + c + LANES] for c in range(0, Q_W, LANES)]
    k0 = 2 * Q_W
    ka = head_norm(p[:, k0:k0 + KV_W], gk_ref[...])
    va = p[:, k0 + KV_W:k0 + 2 * KV_W]
    kb = p[:, k0 + 2 * KV_W:k0 + 3 * KV_W]
    vb = p[:, k0 + 3 * KV_W:k0 + 4 * KV_W]
    if rope:
        cos = cos_ref[...]
        sin = sin_ref[...]
        rot = lambda t: t * cos + _rotate_pairs(t) * sin
        ka_r, kb_r = rot(ka), rot(kb)
        qa = [rot(t) for t in qa]
        qb = [rot(t) for t in qb]
    else:
        ka_r, kb_r = ka, kb
        ka_ref[...] = ka
        va_ref[...] = va
        kb_ref[...] = kb
        vb_ref[...] = vb
    scale = LOG2E * HEAD_DIM ** -0.5
    q_ref[...] = jnp.concatenate([t * scale for t in qa + qb], axis=1).astype(BF16)
    k_ref[...] = jnp.concatenate([ka_r, kb_r], axis=1).astype(BF16)
    vt_ref[0] = jnp.concatenate([va, vb], axis=1).T.astype(BF16)


def _projection(x2d, mod3, mod_row, rows_per_batch, tm, w_in_p, gpre, gq, gk, bd, rope_tabs):
    n_rows = x2d.shape[0]
    tiles_per_batch = rows_per_batch // tm
    rope = rope_tabs is not None

    def mod_spec(j):
        return pl.BlockSpec(
            (1, 1, D_MODEL), lambda i: (mod_row(i // tiles_per_batch) * N_MOD + j, 0, 0))

    const = lambda shape: pl.BlockSpec(shape, lambda i: (0, 0))
    in_specs = [
        pl.BlockSpec((tm, D_MODEL), lambda i: (i, 0)),
        mod_spec(0), mod_spec(1),
        const((1, D_MODEL)), const((D_MODEL, IN_W)), const((1, Q_W)), const((1, KV_W)),
        const((2 * LANES, 2 * LANES)),
    ]
    args = [x2d, mod3, mod3, gpre, w_in_p, gq, gk, bd]
    out_specs = [pl.BlockSpec((tm, 2 * Q_W), lambda i: (i, 0)),
                 pl.BlockSpec((tm, 2 * KV_W), lambda i: (i, 0)),
                 pl.BlockSpec((1, 2 * KV_W, tm),
                              lambda i: (i // tiles_per_batch, 0, i % tiles_per_batch))]
    out_shape = [jax.ShapeDtypeStruct((n_rows, 2 * Q_W), BF16),
                 jax.ShapeDtypeStruct((n_rows, 2 * KV_W), BF16),
                 jax.ShapeDtypeStruct((n_rows // rows_per_batch, 2 * KV_W, rows_per_batch), BF16)]
    if rope:
        tab_spec = pl.BlockSpec((tm, LANES), lambda i: (i % tiles_per_batch, 0))
        in_specs += [tab_spec, tab_spec]
        args += list(rope_tabs)
    else:
        out_specs += [pl.BlockSpec((tm, KV_W), lambda i: (i, 0))] * 4
        out_shape += [jax.ShapeDtypeStruct((n_rows, KV_W), F32)] * 4
    return pl.pallas_call(
        functools.partial(_proj_kernel, rope=rope),
        grid=(n_rows // tm,),
        in_specs=in_specs,
        out_specs=out_specs,
        out_shape=out_shape,
        compiler_params=_params("parallel"),
        name="projection_latent" if rope else "projection_context",
    )(*args)


def _pair_scores(chunk, keys, masks, s_ref):
    lane = lax.broadcasted_iota(jnp.int32, chunk.shape, 1)
    zero = jnp.zeros_like(chunk)
    qpair = jnp.concatenate(
        [jnp.where(lane < HEAD_DIM, chunk, zero), jnp.where(lane >= HEAD_DIM, chunk, zero)], axis=0)
    off = 0
    for k, m in zip(keys, masks):
        s = _dot_nt(k, qpair)
        s_ref[off:off + k.shape[0], :] = s if m is None else jnp.where(m, s, NEG_INF)
        off += k.shape[0]


def _pair_softmax(s_ref, sink_row, e_ref):
    s = s_ref[...]
    mx = jnp.max(s, axis=0, keepdims=True)
    if sink_row is not None:
        mx = jnp.maximum(mx, sink_row)
    e = jnp.exp2(s - mx)
    denom = jnp.sum(e, axis=0, keepdims=True)
    if sink_row is not None:
        denom = denom + jnp.exp2(sink_row - mx)
    e_ref[...] = e.astype(BF16)
    return denom


def _pair_output(e_ref, values_t, denom, tq):
    off, o_t = 0, None
    for vt in values_t:
        part = _dot(vt, e_ref[off:off + vt.shape[1], :])
        o_t = part if o_t is None else o_t + part
        off += vt.shape[1]
    o_t = o_t * (1.0 / denom)
    feat = lax.broadcasted_iota(jnp.int32, (LANES, tq), 0)
    return jnp.where(feat < HEAD_DIM, o_t[:, :tq], o_t[:, tq:]).T


def _attn_kernel(sink_ref, q_ref, k_ref, vt_ref, x_ref, gt1_ref, sh2_ref, sc2_ref, gpost_ref,
                 gpre2_ref, wout_ref, wr_hi_ref, wr_lo_ref, *rest, latent, tq):
    n_pairs = Q_W // LANES
    n_chains = 2 * n_pairs
    s_refs, e_refs = rest[-2 * n_chains:-n_chains], rest[-n_chains:]
    rest = rest[:-2 * n_chains]
    if latent:
        cak_ref, cav_ref, cbk_ref, cbv_ref, x1_ref, h2_ref, afft_ref = rest
    else:
        x1_ref, h2_ref, afft_ref = rest
    col = lax.broadcasted_iota(jnp.int32, (1, 2 * tq), 1)
    if latent:
        n_tok = k_ref.shape[0]
        band = 3 * BLOCK
        t0 = pl.program_id(1) * tq
        start = pl.multiple_of(jnp.clip(t0 - BLOCK, 0, n_tok - band), BLOCK)
        keys_a = [cak_ref[0].astype(BF16), k_ref[:, :KV_W]]
        vals_a = [cav_ref[0].T.astype(BF16), vt_ref[0, :KV_W, :]]
        keys_b = [cbk_ref[0].astype(BF16), k_ref[pl.ds(start, band), KV_W:]]
        vals_b = [cbv_ref[0].T.astype(BF16), vt_ref[0, KV_W:, pl.ds(start, band)]]
        kpos = start + lax.broadcasted_iota(jnp.int32, (band, 2 * tq), 0)
        qpos = t0 + (lax.broadcasted_iota(jnp.int32, (band, 2 * tq), 1) & (tq - 1))
        masks_a, masks_b = [None, None], [None, jnp.abs(kpos - qpos) <= WINDOW]
    else:
        keys_a, vals_a = [k_ref[:, :KV_W]], [vt_ref[0, :KV_W, :]]
        keys_b, vals_b = [k_ref[:, KV_W:]], [vt_ref[0, KV_W:, :]]
        masks_a, masks_b = [None], [None]
    def chain(c):
        if c < n_pairs:
            return keys_a, vals_a, masks_a, None
        j = c - n_pairs
        sink_row = jnp.where(col < tq, sink_ref[j], sink_ref[j + n_pairs]) * LOG2E
        return keys_b, vals_b, masks_b, sink_row

    def scores(c):
        keys, _, masks, _ = chain(c)
        _pair_scores(q_ref[:, c * LANES:(c + 1) * LANES], keys, masks, s_refs[c])

    order = [c for j in range(n_pairs) for c in (j, j + n_pairs)]
    chunks = [None] * n_chains
    scores(order[0])
    for i, c in enumerate(order):
        if i + 1 < n_chains:
            scores(order[i + 1])
        _, vals, _, sink_row = chain(c)
        denom = _pair_softmax(s_refs[c], sink_row, e_refs[c])
        chunks[c] = _pair_output(e_refs[c], vals, denom, tq)
    o = jnp.concatenate(chunks, axis=1)
    proj = _dot(o.astype(BF16), wout_ref[...])
    x1 = x_ref[...] + gt1_ref[0] * (_rms(proj) * gpost_ref[...])
    x1_ref[...] = x1
    h2 = (_rms(x1) * gpre2_ref[...]) * (1.0 + sc2_ref[0]) + sh2_ref[0]
    h2_ref[...] = h2.astype(BF16)
    h_hi, h_lo = _split_bf16(h2)
    wr_hi = wr_hi_ref[...]
    logits = _dot(h_hi, wr_hi) + _dot(h_lo, wr_hi) + _dot(h_hi, wr_lo_ref[...])
    lane = lax.broadcasted_iota(jnp.int32, logits.shape, 1)
    logits = jnp.where(lane < N_EXPERTS, logits, NEG_INF)
    ex = jnp.exp(logits - jnp.max(logits, axis=-1, keepdims=True))
    aff = ex * (1.0 / jnp.sum(ex, axis=-1, keepdims=True))
    afft_ref[0] = aff.T[:N_EXPERTS]


def _attention(sink_p, q, k, vt, x2d, mod3, mod_row, n_batch, n_tok, tq, gpost, gpre2, wout_p,
               wr_hi, wr_lo, caches):
    latent = caches is not None
    nq = n_tok // tq
    row = lambda b, i: (b * nq + i, 0)

    def mod_spec(j):
        return pl.BlockSpec((1, 1, D_MODEL), lambda b, i: (mod_row(b) * N_MOD + j, 0, 0))

    const = lambda shape: pl.BlockSpec(shape, lambda b, i: (0, 0))
    in_specs = [
        pl.BlockSpec(memory_space=pltpu.SMEM),
        pl.BlockSpec((tq, 2 * Q_W), row),
        pl.BlockSpec((n_tok, 2 * KV_W), lambda b, i: (b, 0)),
        pl.BlockSpec((1, 2 * KV_W, n_tok), lambda b, i: (b, 0, 0)),
        pl.BlockSpec((tq, D_MODEL), row),
        mod_spec(2), mod_spec(3), mod_spec(4),
        const((1, D_MODEL)), const((1, D_MODEL)),
        const((2 * Q_W, D_MODEL)), const((D_MODEL, LANES)), const((D_MODEL, LANES)),
    ]
    args = [sink_p, q, k, vt, x2d, mod3, mod3, mod3, gpost, gpre2, wout_p, wr_hi, wr_lo]
    if latent:
        cache_spec = pl.BlockSpec((1,) + caches[0].shape[1:], lambda b, i: (b, 0, 0))
        in_specs += [cache_spec] * 4
        args += list(caches)
    out_specs = [
        pl.BlockSpec((tq, D_MODEL), row),
        pl.BlockSpec((tq, D_MODEL), row),
        pl.BlockSpec((1, N_EXPERTS, tq), lambda b, i: (b, 0, i)),
    ]
    out_shape = [
        jax.ShapeDtypeStruct((n_batch * n_tok, D_MODEL), F32),
        jax.ShapeDtypeStruct((n_batch * n_tok, D_MODEL), BF16),
        jax.ShapeDtypeStruct((n_batch, N_EXPERTS, n_tok), F32),
    ]
    n_ctx_keys = caches[0].shape[1] if latent else 0
    keys_a = n_ctx_keys + n_tok
    keys_b = n_ctx_keys + (3 * BLOCK if latent else n_tok)
    n_pairs = Q_W // LANES
    key_counts = [keys_a] * n_pairs + [keys_b] * n_pairs
    scratch = ([pltpu.VMEM((n, 2 * tq), F32) for n in key_counts]
               + [pltpu.VMEM((n, 2 * tq), BF16) for n in key_counts])
    return pl.pallas_call(
        functools.partial(_attn_kernel, latent=latent, tq=tq),
        grid=(n_batch, nq),
        in_specs=in_specs,
        out_specs=out_specs,
        out_shape=out_shape,
        scratch_shapes=scratch,
        compiler_params=_params("parallel", "parallel"),
        name="attention_latent" if latent else "attention_context",
    )(*args)


def _route_kernel(afft_ref, tri_ref, slot_ref, *, cap):
    a = afft_ref[...]
    capf = float(cap)
    count_ge = lambda t: jnp.sum(jnp.where(a >= t, 1.0, 0.0), axis=1, keepdims=True)
    p = jnp.full((a.shape[0], 1), 2.0, F32)
    for step in EXP_STEPS:
        cand = p * (2.0 ** -step)
        p = jnp.where(count_ge(cand) < capf, cand, p)
    lo0 = p * 0.5

    def refine(_, carry):
        lo, delta = carry
        cand = lo + delta
        return jnp.where(count_ge(cand) >= capf, cand, lo), delta * 0.5

    thr, _ = lax.fori_loop(0, MANTISSA_STEPS, refine, (lo0, lo0 * 0.5))
    above = a > thr
    tied = a == thr
    need = capf - jnp.sum(jnp.where(above, 1.0, 0.0), axis=1, keepdims=True)
    tri = tri_ref[...]
    tie_rank = _dot(jnp.where(tied, 1.0, 0.0).astype(BF16), tri)
    sel = jnp.logical_or(above, jnp.logical_and(tied, tie_rank < need))
    slot = _dot(jnp.where(sel, 1.0, 0.0).astype(BF16), tri)
    slot_ref[...] = jnp.where(sel, slot, -1.0).astype(jnp.int32)


def _route(afft2d, tri, cap):
    rows, n_tok = afft2d.shape
    return pl.pallas_call(
        functools.partial(_route_kernel, cap=cap),
        grid=(1,),
        in_specs=[pl.BlockSpec((rows, n_tok), lambda i: (0, 0)),
                  pl.BlockSpec((n_tok, n_tok), lambda i: (0, 0))],
        out_specs=pl.BlockSpec((rows, n_tok), lambda i: (0, 0)),
        out_shape=jax.ShapeDtypeStruct((rows, n_tok), jnp.int32),
        compiler_params=_params("arbitrary"),
        name="route",
    )(afft2d, tri)


def _one_hot_rows(slot_row, cap):
    return slot_row == lax.broadcasted_iota(jnp.int32, (cap, slot_row.shape[1]), 0)


def _gather_kernel(slot_ref, afft_ref, h_ref, xg_ref, g_ref, *, cap):
    h = h_ref[...]
    for e in range(N_EXPERTS):
        onehot = _one_hot_rows(slot_ref[e:e + 1, :], cap)
        xg_ref[e] = _dot(jnp.where(onehot, 1.0, 0.0).astype(BF16), h).astype(BF16)
        g = jnp.sum(jnp.where(onehot, afft_ref[0, e:e + 1, :], 0.0), axis=1, keepdims=True)
        g_ref[e] = jnp.broadcast_to(g, (cap, LANES))


def _gather(slot2d, afft, h2, n_batch, n_tok, cap):
    return pl.pallas_call(
        functools.partial(_gather_kernel, cap=cap),
        grid=(n_batch,),
        in_specs=[
            pl.BlockSpec((N_EXPERTS, n_tok), lambda b: (b, 0)),
            pl.BlockSpec((1, N_EXPERTS, n_tok), lambda b: (b, 0, 0)),
            pl.BlockSpec((n_tok, D_MODEL), lambda b: (b, 0)),
        ],
        out_specs=[
            pl.BlockSpec((N_EXPERTS, cap, D_MODEL), lambda b: (0, b, 0)),
            pl.BlockSpec((N_EXPERTS, cap, LANES), lambda b: (0, b, 0)),
        ],
        out_shape=[
            jax.ShapeDtypeStruct((N_EXPERTS, n_batch * cap, D_MODEL), BF16),
            jax.ShapeDtypeStruct((N_EXPERTS, n_batch * cap, LANES), F32),
        ],
        compiler_params=_params("parallel"),
        name="gather",
    )(slot2d, afft, h2)


def _ffn_kernel(xc_ref, xl_ref, gc_ref, gl_ref, wg_ref, wu_ref, wd_ref, yc_ref, yl_ref,
                acc_c, acc_l):
    f = pl.program_id(1)
    wg = wg_ref[0].astype(BF16)
    wu = wu_ref[0].astype(BF16)
    wd = wd_ref[0].astype(BF16)

    @pl.when(f == 0)
    def _():
        acc_c[...] = jnp.zeros_like(acc_c)
        acc_l[...] = jnp.zeros_like(acc_l)

    for x_ref, acc in ((xc_ref, acc_c), (xl_ref, acc_l)):
        x = x_ref[0]
        act = (_silu(_dot(x, wg)) * _dot(x, wu)).astype(BF16)
        acc[...] += _dot(act, wd)

    @pl.when(f == pl.num_programs(1) - 1)
    def _():
        for acc, g_ref, y_ref in ((acc_c, gc_ref, yc_ref), (acc_l, gl_ref, yl_ref)):
            y_ref[0] = (acc[...] * jnp.tile(g_ref[0], (1, D_MODEL // LANES))).astype(BF16)


def _ffn(xc, xl, gc, gl, w_gate, w_up, w_down):
    rows_c, rows_l = xc.shape[1], xl.shape[1]
    d_ff = w_gate.shape[2]
    fc = FFN_HIDDEN_CHUNK
    row_map = lambda e, f: (e, 0, 0)
    return pl.pallas_call(
        _ffn_kernel,
        grid=(N_EXPERTS, d_ff // fc),
        in_specs=[
            pl.BlockSpec((1, rows_c, D_MODEL), row_map),
            pl.BlockSpec((1, rows_l, D_MODEL), row_map),
            pl.BlockSpec((1, rows_c, LANES), row_map),
            pl.BlockSpec((1, rows_l, LANES), row_map),
            pl.BlockSpec((1, D_MODEL, fc), lambda e, f: (e, 0, f)),
            pl.BlockSpec((1, D_MODEL, fc), lambda e, f: (e, 0, f)),
            pl.BlockSpec((1, fc, D_MODEL), lambda e, f: (e, f, 0)),
        ],
        out_specs=[
            pl.BlockSpec((1, rows_c, D_MODEL), row_map),
            pl.BlockSpec((1, rows_l, D_MODEL), row_map),
        ],
        out_shape=[jax.ShapeDtypeStruct(xc.shape, BF16), jax.ShapeDtypeStruct(xl.shape, BF16)],
        scratch_shapes=[pltpu.VMEM((rows_c, D_MODEL), F32), pltpu.VMEM((rows_l, D_MODEL), F32)],
        compiler_params=_params("arbitrary", "arbitrary"),
        name="expert_ffn",
    )(xc, xl, gc, gl, w_gate, w_up, w_down)


def _scatter_kernel(slot_ref, y_ref, x1_ref, gt2_ref, gpost_ref, out_ref, *, cap):
    onehot = jnp.concatenate(
        [jnp.where(_one_hot_rows(slot_ref[e:e + 1, :], cap), 1.0, 0.0).astype(BF16)
         for e in range(N_EXPERTS)], axis=0)
    y = y_ref[...].reshape(N_EXPERTS * cap, D_MODEL)
    moe = _dot_tn(onehot, y)
    out_ref[...] = x1_ref[...] + gt2_ref[0] * (_rms(moe) * gpost_ref[...])


def _scatter(slot2d, y, x1, mod3, mod_row, n_batch, n_tok, cap, gpost):
    return pl.pallas_call(
        functools.partial(_scatter_kernel, cap=cap),
        grid=(n_batch,),
        in_specs=[
            pl.BlockSpec((N_EXPERTS, n_tok), lambda b: (b, 0)),
            pl.BlockSpec((N_EXPERTS, cap, D_MODEL), lambda b: (0, b, 0)),
            pl.BlockSpec((n_tok, D_MODEL), lambda b: (b, 0)),
            pl.BlockSpec((1, 1, D_MODEL), lambda b: (mod_row(b) * N_MOD + 5, 0, 0)),
            pl.BlockSpec((1, D_MODEL), lambda b: (0, 0)),
        ],
        out_specs=pl.BlockSpec((n_tok, D_MODEL), lambda b: (b, 0)),
        out_shape=jax.ShapeDtypeStruct((n_batch * n_tok, D_MODEL), F32),
        compiler_params=_params("parallel"),
        name="scatter",
    )(slot2d, y, x1, mod3, gpost)


def _pair_heads(w, axis):
    shape = w.shape
    w = w.reshape(shape[:axis] + (2, 4, HEAD_DIM) + shape[axis + 1:])
    w = jnp.swapaxes(w, axis, axis + 1)
    return w.reshape(shape)


def _rope_tables(n_tok):
    half = HEAD_DIM // 4
    freqs = ROPE_THETA ** (-jnp.arange(half, dtype=F32) / half)
    rows = jnp.repeat(jnp.arange(n_tok // GRID_W, dtype=F32), GRID_W)
    cols = jnp.tile(jnp.arange(GRID_W, dtype=F32), n_tok // GRID_W)
    d = jnp.arange(HEAD_DIM)
    pos = jnp.where((d // (2 * half))[None, :] == 0, rows[:, None], cols[:, None])
    ang = pos * freqs[d % half][None, :]
    sign = jnp.where((d % (2 * half)) < half, -1.0, 1.0).astype(F32)
    cos = jnp.tile(jnp.cos(ang), (1, LANES // HEAD_DIM))
    sin = jnp.tile(jnp.sin(ang) * sign[None, :], (1, LANES // HEAD_DIM))
    return cos, sin


def _prefix_matrix(n):
    r = jnp.arange(n)
    return (r[:, None] < r[None, :]).astype(BF16)


def kernel(x_prompt, x_sample, cache_a_k, cache_a_v, cache_b_k, cache_b_v, c, c_ctx, w_ada, b_ada,
           g_pre_attn, w_in, g_qa, g_ka, sink_b, w_out, g_post_attn, g_pre_ffn, w_router, w_gate,
           w_up, w_down, g_post_ffn):
    n_ctx, t_ctx, _ = x_prompt.shape
    n_lat, t_lat, _ = x_sample.shape
    assert w_ada.shape[0] == 1, "single-layer step"
    assert n_lat + 1 <= MOD_ROWS

    w_in0 = w_in[0]
    c1, c2, c3, c4, c5 = Q_W, Q_W + KV_W, Q_W + 2 * KV_W, 2 * Q_W + 2 * KV_W, 2 * Q_W + 3 * KV_W
    w_in_p = jnp.concatenate(
        [_pair_heads(w_in0[:, :c1], 1), _pair_heads(w_in0[:, c3:c4], 1),
         w_in0[:, c1:c3], w_in0[:, c4:]], axis=1).astype(BF16)
    w_out0 = w_out[0]
    w_out_p = jnp.concatenate(
        [_pair_heads(w_out0[:Q_W], 0), _pair_heads(w_out0[Q_W:], 0)], axis=0).astype(BF16)
    wr = jnp.pad(w_router[0], ((0, 0), (0, LANES - N_EXPERTS)))
    wr_hi = wr.astype(BF16)
    wr_lo = (wr - wr_hi.astype(F32)).astype(BF16)
    gq = jnp.tile(g_qa[0], N_HEADS)[None, :]
    gk = jnp.tile(g_ka[0], N_KV)[None, :]
    blk = jnp.arange(2 * LANES) // HEAD_DIM
    bd = (blk[:, None] == blk[None, :]).astype(BF16)
    sink = sink_b[0].astype(F32)
    cond = jnp.zeros((MOD_ROWS, D_MODEL), F32).at[:n_lat].set(c).at[n_lat].set(c_ctx)

    mod = _modulation(cond, w_ada[0], b_ada[0][None, :])
    mod3 = mod.reshape(MOD_ROWS * N_MOD, 1, D_MODEL)

    xc = x_prompt.reshape(n_ctx * t_ctx, D_MODEL)
    xl = x_sample.reshape(n_lat * t_lat, D_MODEL)
    gpre1, gpost1 = g_pre_attn[0][None, :], g_post_attn[0][None, :]
    gpre2, gpost2 = g_pre_ffn[0][None, :], g_post_ffn[0][None, :]

    ctx_row = lambda b: n_lat
    lat_row = lambda b: b

    qc, kc, vtc, ka, va, kb, vb = _projection(
        xc, mod3, ctx_row, t_ctx, t_ctx, w_in_p, gpre1, gq, gk, bd, None)
    x1c, h2c, afftc = _attention(
        sink, qc, kc, vtc, xc, mod3, ctx_row, n_ctx, t_ctx, t_ctx, gpost1, gpre2, w_out_p,
        wr_hi, wr_lo, None)

    ql, kl, vtl = _projection(
        xl, mod3, lat_row, t_lat, 256, w_in_p, gpre1, gq, gk, bd, _rope_tables(t_lat))
    caches = [t[:, 0].reshape(n_lat, t.shape[2], KV_W)
              for t in (cache_a_k, cache_a_v, cache_b_k, cache_b_v)]
    x1l, h2l, afftl = _attention(
        sink, ql, kl, vtl, xl, mod3, lat_row, n_lat, t_lat, BLOCK, gpost1, gpre2, w_out_p,
        wr_hi, wr_lo, caches)

    cap_c = CAPACITY_FACTOR * t_ctx // N_EXPERTS
    cap_l = CAPACITY_FACTOR * t_lat // N_EXPERTS
    slot_c = _route(afftc.reshape(n_ctx * N_EXPERTS, t_ctx), _prefix_matrix(t_ctx), cap_c)
    slot_l = _route(afftl.reshape(n_lat * N_EXPERTS, t_lat), _prefix_matrix(t_lat), cap_l)
    xgc, gc = _gather(slot_c, afftc, h2c, n_ctx, t_ctx, cap_c)
    xgl, gl = _gather(slot_l, afftl, h2l, n_lat, t_lat, cap_l)
    yc, yl = _ffn(xgc, xgl, gc, gl, w_gate[0], w_up[0], w_down[0])
    y_prompt = _scatter(slot_c, yc, x1c, mod3, ctx_row, n_ctx, t_ctx, cap_c, gpost2)
    y_sample = _scatter(slot_l, yl, x1l, mod3, lat_row, n_lat, t_lat, cap_l, gpost2)

    kv_shape = (n_ctx, 1, t_ctx, N_KV, HEAD_DIM)
    return (y_prompt.reshape(x_prompt.shape), y_sample.reshape(x_sample.shape),
            ka.reshape(kv_shape), va.reshape(kv_shape), kb.reshape(kv_shape), vb.reshape(kv_shape))
```

```python
import functools

import jax
import jax.numpy as jnp
from jax import lax
from jax.experimental import pallas as pl
from jax.experimental.pallas import tpu as pltpu

F32 = jnp.float32
BF16 = jnp.bfloat16

D_MODEL = 1024
HEAD_DIM = 64
N_HEADS = 8
N_KV = 2
Q_W = N_HEADS * HEAD_DIM
KV_W = N_KV * HEAD_DIM
IN_W = 2 * (Q_W + 2 * KV_W)
GRID_W = 64
BLOCK = 128
WINDOW = 128
ROPE_THETA = 10000.0
N_EXPERTS = 16
CAPACITY_FACTOR = 2
N_MOD = 6
EPS = 1e-6
NEG_INF = -1e30
LOG2E = 1.4426950408889634
LANES = 128
MOD_ROWS = 16
VMEM_LIMIT = 56 * 1024 * 1024
FFN_HIDDEN_CHUNK = 256
LATENT_Q_TILE = 256
LATENT_PROJ_TILE = 512

EXP_STEPS = (64, 32, 16, 8, 4, 2, 1)
MANTISSA_STEPS = 36


def _dot(a, b):
    return jnp.dot(a, b, preferred_element_type=F32)


def _dot_nt(a, b):
    return lax.dot_general(a, b, (((1,), (1,)), ((), ())), preferred_element_type=F32)


def _dot_tn(a, b):
    return lax.dot_general(a, b, (((0,), (0,)), ((), ())), preferred_element_type=F32)


def _split_bf16(x):
    hi = x.astype(BF16)
    lo = (x - hi.astype(F32)).astype(BF16)
    return hi, lo


def _rms(x):
    return x * lax.rsqrt(jnp.mean(x * x, axis=-1, keepdims=True) + EPS)


def _silu(x):
    return x * (1.0 / (1.0 + jnp.exp(-x)))


def _params(*sem):
    return pltpu.CompilerParams(dimension_semantics=sem, vmem_limit_bytes=VMEM_LIMIT)


def _mod_kernel(cond_ref, w_ref, b_ref, out_ref):
    a_hi, a_lo = _split_bf16(_silu(cond_ref[...]))
    w_hi, w_lo = _split_bf16(w_ref[...])
    out_ref[...] = _dot(a_hi, w_hi) + _dot(a_hi, w_lo) + _dot(a_lo, w_hi) + b_ref[...]


def _modulation(cond, w_ada, b_ada):
    n = w_ada.shape[1]
    tn = 1536
    return pl.pallas_call(
        _mod_kernel,
        grid=(n // tn,),
        in_specs=[
            pl.BlockSpec((MOD_ROWS, D_MODEL), lambda j: (0, 0)),
            pl.BlockSpec((D_MODEL, tn), lambda j: (0, j)),
            pl.BlockSpec((1, tn), lambda j: (0, j)),
        ],
        out_specs=pl.BlockSpec((MOD_ROWS, tn), lambda j: (0, j)),
        out_shape=jax.ShapeDtypeStruct((MOD_ROWS, n), F32),
        compiler_params=_params("arbitrary"),
        name="modulation",
    )(cond, w_ada, b_ada)


def _rotate_pairs(x):
    lane = lax.broadcasted_iota(jnp.int32, x.shape, 1)
    return jnp.where((lane & 31) < 16, pltpu.roll(x, LANES - 16, 1), pltpu.roll(x, 16, 1))


def _proj_kernel(x_ref, sh_ref, sc_ref, gpre_ref, w_ref, gq_ref, gk_ref, bd_ref, *rest, rope):
    if rope:
        cos_ref, sin_ref, q_ref, k_ref, vt_ref = rest
    else:
        q_ref, k_ref, vt_ref, ka_ref, va_ref, kb_ref, vb_ref = rest
    h = _rms(x_ref[...]) * gpre_ref[...]
    h = h * (1.0 + sc_ref[0]) + sh_ref[0]
    p = _dot(h.astype(BF16), w_ref[...])
    bd = bd_ref[...]

    def head_norm(t, g):
        ss = _dot((t * t).astype(BF16), bd[: t.shape[1], : t.shape[1]])
        return t * lax.rsqrt(ss * (1.0 / HEAD_DIM) + EPS) * g

    gq = gq_ref[...]
    qa = [head_norm(p[:, c:c + 2 * LANES], gq[:, c:c + 2 * LANES]) for c in (0, 2 * LANES)]
    qa = [t[:, c:c + LANES] for t in qa for c in (0, LANES)]
    qb = [p[:, Q_W + c:Q_W + c + LANES] for c in range(0, Q_W, LANES)]
    k0 = 2 * Q_W
    ka = head_norm(p[:, k0:k0 + KV_W], gk_ref[...])
    va = p[:, k0 + KV_W:k0 + 2 * KV_W]
    kb = p[:, k0 + 2 * KV_W:k0 + 3 * KV_W]
    vb = p[:, k0 + 3 * KV_W:k0 + 4 * KV_W]
    if rope:
        cos = cos_ref[...]
        sin = sin_ref[...]
        rot = lambda t: t * cos + _rotate_pairs(t) * sin
        ka_r, kb_r = rot(ka), rot(kb)
        qa = [rot(t) for t in qa]
        qb = [rot(t) for t in qb]
    else:
        ka_r, kb_r = ka, kb
        ka_ref[...] = ka
        va_ref[...] = va
        kb_ref[...] = kb
        vb_ref[...] = vb
    scale = LOG2E * HEAD_DIM ** -0.5
    q_ref[...] = jnp.concatenate([t * scale for t in qa + qb], axis=1).astype(BF16)
    k_ref[...] = jnp.concatenate([ka_r, kb_r], axis=1).astype(BF16)
    vt_ref[0] = jnp.concatenate([va, vb], axis=1).T.astype(BF16)


def _projection(x2d, mod3, mod_row, rows_per_batch, tm, w_in_p, gpre, gq, gk, bd, rope_tabs):
    n_rows = x2d.shape[0]
    tiles_per_batch = rows_per_batch // tm
    rope = rope_tabs is not None

    def mod_spec(j):
        return pl.BlockSpec(
            (1, 1, D_MODEL), lambda i: (mod_row(i // tiles_per_batch) * N_MOD + j, 0, 0))

    const = lambda shape: pl.BlockSpec(shape, lambda i: (0, 0))
    in_specs = [
        pl.BlockSpec((tm, D_MODEL), lambda i: (i, 0)),
        mod_spec(0), mod_spec(1),
        const((1, D_MODEL)), const((D_MODEL, IN_W)), const((1, Q_W)), const((1, KV_W)),
        const((2 * LANES, 2 * LANES)),
    ]
    args = [x2d, mod3, mod3, gpre, w_in_p, gq, gk, bd]
    out_specs = [pl.BlockSpec((tm, 2 * Q_W), lambda i: (i, 0)),
                 pl.BlockSpec((tm, 2 * KV_W), lambda i: (i, 0)),
                 pl.BlockSpec((1, 2 * KV_W, tm),
                              lambda i: (i // tiles_per_batch, 0, i % tiles_per_batch))]
    out_shape = [jax.ShapeDtypeStruct((n_rows, 2 * Q_W), BF16),
                 jax.ShapeDtypeStruct((n_rows, 2 * KV_W), BF16),
                 jax.ShapeDtypeStruct((n_rows // rows_per_batch, 2 * KV_W, rows_per_batch), BF16)]
    if rope:
        tab_spec = pl.BlockSpec((tm, LANES), lambda i: (i % tiles_per_batch, 0))
        in_specs += [tab_spec, tab_spec]
        args += list(rope_tabs)
    else:
        out_specs += [pl.BlockSpec((tm, KV_W), lambda i: (i, 0))] * 4
        out_shape += [jax.ShapeDtypeStruct((n_rows, KV_W), F32)] * 4
    return pl.pallas_call(
        functools.partial(_proj_kernel, rope=rope),
        grid=(n_rows // tm,),
        in_specs=in_specs,
        out_specs=out_specs,
        out_shape=out_shape,
        compiler_params=_params("parallel"),
        name="projection_latent" if rope else "projection_context",
    )(*args)


def _band_keys(tq):
    assert WINDOW % LANES == 0 and tq % LANES == 0
    return tq + 2 * WINDOW


def _pair_scores(chunk, keys, masks, s_ref):
    lane = lax.broadcasted_iota(jnp.int32, chunk.shape, 1)
    zero = jnp.zeros_like(chunk)
    qpair = jnp.concatenate(
        [jnp.where(lane < HEAD_DIM, chunk, zero), jnp.where(lane >= HEAD_DIM, chunk, zero)], axis=0)
    off = 0
    for k, m in zip(keys, masks):
        s = _dot_nt(k, qpair)
        s_ref[off:off + k.shape[0], :] = s if m is None else jnp.where(m, s, NEG_INF)
        off += k.shape[0]


def _pair_softmax(s_ref, sink_row, e_ref):
    s = s_ref[...]
    mx = jnp.max(s, axis=0, keepdims=True)
    if sink_row is not None:
        mx = jnp.maximum(mx, sink_row)
    e = jnp.exp2(s - mx)
    denom = jnp.sum(e, axis=0, keepdims=True)
    if sink_row is not None:
        denom = denom + jnp.exp2(sink_row - mx)
    e_ref[...] = e.astype(BF16)
    return denom


def _pair_output(e_ref, values_t, denom, tq):
    off, o_t = 0, None
    for vt in values_t:
        part = _dot(vt, e_ref[off:off + vt.shape[1], :])
        o_t = part if o_t is None else o_t + part
        off += vt.shape[1]
    o_t = o_t * (1.0 / denom)
    feat = lax.broadcasted_iota(jnp.int32, (LANES, tq), 0)
    return jnp.where(feat < HEAD_DIM, o_t[:, :tq], o_t[:, tq:]).T


def _attn_kernel(sink_ref, q_ref, k_ref, vt_ref, x_ref, gt1_ref, sh2_ref, sc2_ref, gpost_ref,
                 gpre2_ref, wout_ref, wr_hi_ref, wr_lo_ref, *rest, latent, tq):
    n_pairs = Q_W // LANES
    n_chains = 2 * n_pairs
    s_refs, e_refs = rest[-2 * n_chains:-n_chains], rest[-n_chains:]
    rest = rest[:-2 * n_chains]
    if latent:
        cak_ref, cav_ref, cbk_ref, cbv_ref, x1_ref, h2_ref, afft_ref = rest
    else:
        x1_ref, h2_ref, afft_ref = rest
    col = lax.broadcasted_iota(jnp.int32, (1, 2 * tq), 1)
    if latent:
        n_tok = k_ref.shape[0]
        band = _band_keys(tq)
        t0 = pl.program_id(1) * tq
        start = pl.multiple_of(jnp.clip(t0 - WINDOW, 0, n_tok - band), LANES)
        flat = lambda ref: jnp.concatenate([ref[0, 0, :, hd, :] for hd in range(N_KV)], axis=1)
        keys_a = [flat(cak_ref).astype(BF16), k_ref[:, :KV_W]]
        vals_a = [flat(cav_ref).T.astype(BF16), vt_ref[0, :KV_W, :]]
        keys_b = [flat(cbk_ref).astype(BF16), k_ref[pl.ds(start, band), KV_W:]]
        vals_b = [flat(cbv_ref).T.astype(BF16), vt_ref[0, KV_W:, pl.ds(start, band)]]
        kpos = start + lax.broadcasted_iota(jnp.int32, (band, 2 * tq), 0)
        qpos = t0 + (lax.broadcasted_iota(jnp.int32, (band, 2 * tq), 1) & (tq - 1))
        masks_a, masks_b = [None, None], [None, jnp.abs(kpos - qpos) <= WINDOW]
    else:
        keys_a, vals_a = [k_ref[:, :KV_W]], [vt_ref[0, :KV_W, :]]
        keys_b, vals_b = [k_ref[:, KV_W:]], [vt_ref[0, KV_W:, :]]
        masks_a, masks_b = [None], [None]
    def chain(c):
        if c < n_pairs:
            return keys_a, vals_a, masks_a, None
        j = c - n_pairs
        sink_row = jnp.where(col < tq, sink_ref[j], sink_ref[j + n_pairs]) * LOG2E
        return keys_b, vals_b, masks_b, sink_row

    def scores(c):
        keys, _, masks, _ = chain(c)
        _pair_scores(q_ref[:, c * LANES:(c + 1) * LANES], keys, masks, s_refs[c])

    order = [c for j in range(n_pairs) for c in (j, j + n_pairs)]
    chunks = [None] * n_chains
    scores(order[0])
    for i, c in enumerate(order):
        if i + 1 < n_chains:
            scores(order[i + 1])
        _, vals, _, sink_row = chain(c)
        denom = _pair_softmax(s_refs[c], sink_row, e_refs[c])
        chunks[c] = _pair_output(e_refs[c], vals, denom, tq)
    o = jnp.concatenate(chunks, axis=1)
    proj = _dot(o.astype(BF16), wout_ref[...])
    x1 = x_ref[...] + gt1_ref[0] * (_rms(proj) * gpost_ref[...])
    x1_ref[...] = x1
    h2 = (_rms(x1) * gpre2_ref[...]) * (1.0 + sc2_ref[0]) + sh2_ref[0]
    h2_ref[...] = h2.astype(BF16)
    h_hi, h_lo = _split_bf16(h2)
    wr_hi = wr_hi_ref[...]
    logits = _dot(h_hi, wr_hi) + _dot(h_lo, wr_hi) + _dot(h_hi, wr_lo_ref[...])
    lane = lax.broadcasted_iota(jnp.int32, logits.shape, 1)
    logits = jnp.where(lane < N_EXPERTS, logits, NEG_INF)
    ex = jnp.exp(logits - jnp.max(logits, axis=-1, keepdims=True))
    aff = ex * (1.0 / jnp.sum(ex, axis=-1, keepdims=True))
    afft_ref[0] = aff.T[:N_EXPERTS]


def _attention(sink_p, q, k, vt, x2d, mod3, mod_row, n_batch, n_tok, tq, gpost, gpre2, wout_p,
               wr_hi, wr_lo, caches):
    latent = caches is not None
    nq = n_tok // tq
    row = lambda b, i: (b * nq + i, 0)

    def mod_spec(j):
        return pl.BlockSpec((1, 1, D_MODEL), lambda b, i: (mod_row(b) * N_MOD + j, 0, 0))

    const = lambda shape: pl.BlockSpec(shape, lambda b, i: (0, 0))
    in_specs = [
        pl.BlockSpec(memory_space=pltpu.SMEM),
        pl.BlockSpec((tq, 2 * Q_W), row),
        pl.BlockSpec((n_tok, 2 * KV_W), lambda b, i: (b, 0)),
        pl.BlockSpec((1, 2 * KV_W, n_tok), lambda b, i: (b, 0, 0)),
        pl.BlockSpec((tq, D_MODEL), row),
        mod_spec(2), mod_spec(3), mod_spec(4),
        const((1, D_MODEL)), const((1, D_MODEL)),
        const((2 * Q_W, D_MODEL)), const((D_MODEL, LANES)), const((D_MODEL, LANES)),
    ]
    args = [sink_p, q, k, vt, x2d, mod3, mod3, mod3, gpost, gpre2, wout_p, wr_hi, wr_lo]
    if latent:
        cache_spec = pl.BlockSpec((1,) + caches[0].shape[1:], lambda b, i: (b, 0, 0, 0, 0))
        in_specs += [cache_spec] * 4
        args += list(caches)
    out_specs = [
        pl.BlockSpec((tq, D_MODEL), row),
        pl.BlockSpec((tq, D_MODEL), row),
        pl.BlockSpec((1, N_EXPERTS, tq), lambda b, i: (b, 0, i)),
    ]
    out_shape = [
        jax.ShapeDtypeStruct((n_batch * n_tok, D_MODEL), F32),
        jax.ShapeDtypeStruct((n_batch * n_tok, D_MODEL), BF16),
        jax.ShapeDtypeStruct((n_batch, N_EXPERTS, n_tok), F32),
    ]
    n_ctx_keys = caches[0].shape[2] if latent else 0
    keys_a = n_ctx_keys + n_tok
    keys_b = n_ctx_keys + (_band_keys(tq) if latent else n_tok)
    n_pairs = Q_W // LANES
    key_counts = [keys_a] * n_pairs + [keys_b] * n_pairs
    scratch = ([pltpu.VMEM((n, 2 * tq), F32) for n in key_counts]
               + [pltpu.VMEM((n, 2 * tq), BF16) for n in key_counts])
    return pl.pallas_call(
        functools.partial(_attn_kernel, latent=latent, tq=tq),
        grid=(n_batch, nq),
        in_specs=in_specs,
        out_specs=out_specs,
        out_shape=out_shape,
        scratch_shapes=scratch,
        compiler_params=_params("parallel", "parallel"),
        name="attention_latent" if latent else "attention_context",
    )(*args)


def _route_kernel(afft_ref, tri_ref, slot_ref, *, cap):
    a = afft_ref[...]
    capf = float(cap)
    count_ge = lambda t: jnp.sum(jnp.where(a >= t, 1.0, 0.0), axis=1, keepdims=True)
    p = jnp.full((a.shape[0], 1), 2.0, F32)
    for step in EXP_STEPS:
        cand = p * (2.0 ** -step)
        p = jnp.where(count_ge(cand) < capf, cand, p)
    lo0 = p * 0.5

    def refine(_, carry):
        lo, delta = carry
        cand = lo + delta
        return jnp.where(count_ge(cand) >= capf, cand, lo), delta * 0.5

    thr, _ = lax.fori_loop(0, MANTISSA_STEPS, refine, (lo0, lo0 * 0.5))
    above = a > thr
    tied = a == thr
    need = capf - jnp.sum(jnp.where(above, 1.0, 0.0), axis=1, keepdims=True)
    tri = tri_ref[...]
    tie_rank = _dot(jnp.where(tied, 1.0, 0.0).astype(BF16), tri)
    sel = jnp.logical_or(above, jnp.logical_and(tied, tie_rank < need))
    slot = _dot(jnp.where(sel, 1.0, 0.0).astype(BF16), tri)
    slot_ref[...] = jnp.where(sel, slot, -1.0).astype(jnp.int32)


def _route(afft2d, tri, cap):
    rows, n_tok = afft2d.shape
    return pl.pallas_call(
        functools.partial(_route_kernel, cap=cap),
        grid=(1,),
        in_specs=[pl.BlockSpec((rows, n_tok), lambda i: (0, 0)),
                  pl.BlockSpec((n_tok, n_tok), lambda i: (0, 0))],
        out_specs=pl.BlockSpec((rows, n_tok), lambda i: (0, 0)),
        out_shape=jax.ShapeDtypeStruct((rows, n_tok), jnp.int32),
        compiler_params=_params("arbitrary"),
        name="route",
    )(afft2d, tri)


def _one_hot_rows(slot_row, cap):
    return slot_row == lax.broadcasted_iota(jnp.int32, (cap, slot_row.shape[1]), 0)


def _gather_kernel(slot_ref, afft_ref, h_ref, xg_ref, g_ref, *, cap):
    h = h_ref[...]
    onehots = []
    for e in range(N_EXPERTS):
        onehot = _one_hot_rows(slot_ref[e:e + 1, :], cap)
        onehots.append(jnp.where(onehot, 1.0, 0.0).astype(BF16))
        g = jnp.sum(jnp.where(onehot, afft_ref[0, e:e + 1, :], 0.0), axis=1, keepdims=True)
        g_ref[e] = jnp.broadcast_to(g, (cap, LANES))
    x = _dot(jnp.concatenate(onehots, axis=0), h)
    xg_ref[...] = x.astype(BF16).reshape(N_EXPERTS, cap, D_MODEL)


def _gather(slot2d, afft, h2, n_batch, n_tok, cap):
    return pl.pallas_call(
        functools.partial(_gather_kernel, cap=cap),
        grid=(n_batch,),
        in_specs=[
            pl.BlockSpec((N_EXPERTS, n_tok), lambda b: (b, 0)),
            pl.BlockSpec((1, N_EXPERTS, n_tok), lambda b: (b, 0, 0)),
            pl.BlockSpec((n_tok, D_MODEL), lambda b: (b, 0)),
        ],
        out_specs=[
            pl.BlockSpec((N_EXPERTS, cap, D_MODEL), lambda b: (0, b, 0)),
            pl.BlockSpec((N_EXPERTS, cap, LANES), lambda b: (0, b, 0)),
        ],
        out_shape=[
            jax.ShapeDtypeStruct((N_EXPERTS, n_batch * cap, D_MODEL), BF16),
            jax.ShapeDtypeStruct((N_EXPERTS, n_batch * cap, LANES), F32),
        ],
        compiler_params=_params("parallel"),
        name="gather",
    )(slot2d, afft, h2)


def _ffn_kernel(xc_ref, xl_ref, gc_ref, gl_ref, wg_ref, wu_ref, wd_ref, yc_ref, yl_ref,
                acc_c, acc_l):
    f = pl.program_id(1)
    wg = wg_ref[0].astype(BF16)
    wu = wu_ref[0].astype(BF16)
    wd = wd_ref[0].astype(BF16)

    @pl.when(f == 0)
    def _():
        acc_c[...] = jnp.zeros_like(acc_c)
        acc_l[...] = jnp.zeros_like(acc_l)

    for x_ref, acc in ((xc_ref, acc_c), (xl_ref, acc_l)):
        x = x_ref[0]
        act = (_silu(_dot(x, wg)) * _dot(x, wu)).astype(BF16)
        acc[...] += _dot(act, wd)

    @pl.when(f == pl.num_programs(1) - 1)
    def _():
        for acc, g_ref, y_ref in ((acc_c, gc_ref, yc_ref), (acc_l, gl_ref, yl_ref)):
            y_ref[0] = (acc[...] * jnp.tile(g_ref[0], (1, D_MODEL // LANES))).astype(BF16)


def _ffn(xc, xl, gc, gl, w_gate, w_up, w_down):
    rows_c, rows_l = xc.shape[1], xl.shape[1]
    d_ff = w_gate.shape[2]
    fc = FFN_HIDDEN_CHUNK
    row_map = lambda e, f: (e, 0, 0)
    return pl.pallas_call(
        _ffn_kernel,
        grid=(N_EXPERTS, d_ff // fc),
        in_specs=[
            pl.BlockSpec((1, rows_c, D_MODEL), row_map),
            pl.BlockSpec((1, rows_l, D_MODEL), row_map),
            pl.BlockSpec((1, rows_c, LANES), row_map),
            pl.BlockSpec((1, rows_l, LANES), row_map),
            pl.BlockSpec((1, D_MODEL, fc), lambda e, f: (e, 0, f)),
            pl.BlockSpec((1, D_MODEL, fc), lambda e, f: (e, 0, f)),
            pl.BlockSpec((1, fc, D_MODEL), lambda e, f: (e, f, 0)),
        ],
        out_specs=[
            pl.BlockSpec((1, rows_c, D_MODEL), row_map),
            pl.BlockSpec((1, rows_l, D_MODEL), row_map),
        ],
        out_shape=[jax.ShapeDtypeStruct(xc.shape, BF16), jax.ShapeDtypeStruct(xl.shape, BF16)],
        scratch_shapes=[pltpu.VMEM((rows_c, D_MODEL), F32), pltpu.VMEM((rows_l, D_MODEL), F32)],
        compiler_params=_params("arbitrary", "arbitrary"),
        name="expert_ffn",
    )(xc, xl, gc, gl, w_gate, w_up, w_down)


def _scatter_kernel(slot_ref, y_ref, x1_ref, gt2_ref, gpost_ref, out_ref, *, cap):
    onehot = jnp.concatenate(
        [jnp.where(_one_hot_rows(slot_ref[e:e + 1, :], cap), 1.0, 0.0).astype(BF16)
         for e in range(N_EXPERTS)], axis=0)
    y = y_ref[...].reshape(N_EXPERTS * cap, D_MODEL)
    moe = _dot_tn(onehot, y)
    out_ref[...] = x1_ref[...] + gt2_ref[0] * (_rms(moe) * gpost_ref[...])


def _scatter(slot2d, y, x1, mod3, mod_row, n_batch, n_tok, cap, gpost):
    return pl.pallas_call(
        functools.partial(_scatter_kernel, cap=cap),
        grid=(n_batch,),
        in_specs=[
            pl.BlockSpec((N_EXPERTS, n_tok), lambda b: (b, 0)),
            pl.BlockSpec((N_EXPERTS, cap, D_MODEL), lambda b: (0, b, 0)),
            pl.BlockSpec((n_tok, D_MODEL), lambda b: (b, 0)),
            pl.BlockSpec((1, 1, D_MODEL), lambda b: (mod_row(b) * N_MOD + 5, 0, 0)),
            pl.BlockSpec((1, D_MODEL), lambda b: (0, 0)),
        ],
        out_specs=pl.BlockSpec((n_tok, D_MODEL), lambda b: (b, 0)),
        out_shape=jax.ShapeDtypeStruct((n_batch * n_tok, D_MODEL), F32),
        compiler_params=_params("parallel"),
        name="scatter",
    )(slot2d, y, x1, mod3, gpost)


def _pair_heads(w, axis):
    shape = w.shape
    w = w.reshape(shape[:axis] + (2, 4, HEAD_DIM) + shape[axis + 1:])
    w = jnp.swapaxes(w, axis, axis + 1)
    return w.reshape(shape)


def _rope_tables(n_tok):
    half = HEAD_DIM // 4
    freqs = ROPE_THETA ** (-jnp.arange(half, dtype=F32) / half)
    rows = jnp.repeat(jnp.arange(n_tok // GRID_W, dtype=F32), GRID_W)
    cols = jnp.tile(jnp.arange(GRID_W, dtype=F32), n_tok // GRID_W)
    d = jnp.arange(HEAD_DIM)
    pos = jnp.where((d // (2 * half))[None, :] == 0, rows[:, None], cols[:, None])
    ang = pos * freqs[d % half][None, :]
    sign = jnp.where((d % (2 * half)) < half, -1.0, 1.0).astype(F32)
    cos = jnp.tile(jnp.cos(ang), (1, LANES // HEAD_DIM))
    sin = jnp.tile(jnp.sin(ang) * sign[None, :], (1, LANES // HEAD_DIM))
    return cos, sin


def _prefix_matrix(n):
    r = jnp.arange(n)
    return (r[:, None] < r[None, :]).astype(BF16)


def kernel(x_prompt, x_sample, cache_a_k, cache_a_v, cache_b_k, cache_b_v, c, c_ctx, w_ada, b_ada,
           g_pre_attn, w_in, g_qa, g_ka, sink_b, w_out, g_post_attn, g_pre_ffn, w_router, w_gate,
           w_up, w_down, g_post_ffn):
    n_ctx, t_ctx, _ = x_prompt.shape
    n_lat, t_lat, _ = x_sample.shape
    assert w_ada.shape[0] == 1, "single-layer step"
    assert n_lat + 1 <= MOD_ROWS

    w_in0 = w_in[0]
    c1, c2, c3, c4, c5 = Q_W, Q_W + KV_W, Q_W + 2 * KV_W, 2 * Q_W + 2 * KV_W, 2 * Q_W + 3 * KV_W
    w_in_p = jnp.concatenate(
        [_pair_heads(w_in0[:, :c1], 1), _pair_heads(w_in0[:, c3:c4], 1),
         w_in0[:, c1:c3], w_in0[:, c4:]], axis=1).astype(BF16)
    w_out0 = w_out[0]
    w_out_p = jnp.concatenate(
        [_pair_heads(w_out0[:Q_W], 0), _pair_heads(w_out0[Q_W:], 0)], axis=0).astype(BF16)
    wr = jnp.pad(w_router[0], ((0, 0), (0, LANES - N_EXPERTS)))
    wr_hi = wr.astype(BF16)
    wr_lo = (wr - wr_hi.astype(F32)).astype(BF16)
    gq = jnp.tile(g_qa[0], N_HEADS)[None, :]
    gk = jnp.tile(g_ka[0], N_KV)[None, :]
    blk = jnp.arange(2 * LANES) // HEAD_DIM
    bd = (blk[:, None] == blk[None, :]).astype(BF16)
    sink = sink_b[0].astype(F32)
    cond = jnp.zeros((MOD_ROWS, D_MODEL), F32).at[:n_lat].set(c).at[n_lat].set(c_ctx)

    mod = _modulation(cond, w_ada[0], b_ada[0][None, :])
    mod3 = mod.reshape(MOD_ROWS * N_MOD, 1, D_MODEL)

    xc = x_prompt.reshape(n_ctx * t_ctx, D_MODEL)
    xl = x_sample.reshape(n_lat * t_lat, D_MODEL)
    gpre1, gpost1 = g_pre_attn[0][None, :], g_post_attn[0][None, :]
    gpre2, gpost2 = g_pre_ffn[0][None, :], g_post_ffn[0][None, :]

    ctx_row = lambda b: n_lat
    lat_row = lambda b: b

    qc, kc, vtc, ka, va, kb, vb = _projection(
        xc, mod3, ctx_row, t_ctx, t_ctx, w_in_p, gpre1, gq, gk, bd, None)
    x1c, h2c, afftc = _attention(
        sink, qc, kc, vtc, xc, mod3, ctx_row, n_ctx, t_ctx, t_ctx, gpost1, gpre2, w_out_p,
        wr_hi, wr_lo, None)

    ql, kl, vtl = _projection(
        xl, mod3, lat_row, t_lat, LATENT_PROJ_TILE, w_in_p, gpre1, gq, gk, bd, _rope_tables(t_lat))
    assert cache_a_k.shape[1] == 1, "single-layer step"
    caches = [cache_a_k, cache_a_v, cache_b_k, cache_b_v]
    x1l, h2l, afftl = _attention(
        sink, ql, kl, vtl, xl, mod3, lat_row, n_lat, t_lat, LATENT_Q_TILE, gpost1, gpre2, w_out_p,
        wr_hi, wr_lo, caches)

    cap_c = CAPACITY_FACTOR * t_ctx // N_EXPERTS
    cap_l = CAPACITY_FACTOR * t_lat // N_EXPERTS
    slot_c = _route(afftc.reshape(n_ctx * N_EXPERTS, t_ctx), _prefix_matrix(t_ctx), cap_c)
    slot_l = _route(afftl.reshape(n_lat * N_EXPERTS, t_lat), _prefix_matrix(t_lat), cap_l)
    xgc, gc = _gather(slot_c, afftc, h2c, n_ctx, t_ctx, cap_c)
    xgl, gl = _gather(slot_l, afftl, h2l, n_lat, t_lat, cap_l)
    yc, yl = _ffn(xgc, xgl, gc, gl, w_gate[0], w_up[0], w_down[0])
    y_prompt = _scatter(slot_c, yc, x1c, mod3, ctx_row, n_ctx, t_ctx, cap_c, gpost2)
    y_sample = _scatter(slot_l, yl, x1l, mod3, lat_row, n_lat, t_lat, cap_l, gpost2)

    kv_shape = (n_ctx, 1, t_ctx, N_KV, HEAD_DIM)
    return (y_prompt.reshape(x_prompt.shape), y_sample.reshape(x_sample.shape),
            ka.reshape(kv_shape), va.reshape(kv_shape), kb.reshape(kv_shape), vb.reshape(kv_shape))
```

```python
import functools

import jax
import jax.numpy as jnp
import numpy as np
from jax import lax
from jax.experimental import pallas as pl
from jax.experimental.pallas import tpu as pltpu

F32 = jnp.float32
BF16 = jnp.bfloat16

D_MODEL = 1024
HEAD_DIM = 64
N_HEADS = 8
N_KV = 2
Q_W = N_HEADS * HEAD_DIM
KV_W = N_KV * HEAD_DIM
IN_W = 2 * (Q_W + 2 * KV_W)
GRID_W = 64
BLOCK = 128
WINDOW = 128
ROPE_THETA = 10000.0
N_EXPERTS = 16
CAPACITY_FACTOR = 2
N_MOD = 6
EPS = 1e-6
NEG_INF = -1e30
LOG2E = 1.4426950408889634
LANES = 128
MOD_ROWS = 16
VMEM_LIMIT = 56 * 1024 * 1024
FFN_HIDDEN_CHUNK = 256
LATENT_Q_TILE = 256
LATENT_PROJ_TILE = 512

EXP_STEPS = (64, 32, 16, 8, 4, 2, 1)
MANTISSA_STEPS = 36


def _dot(a, b):
    return jnp.dot(a, b, preferred_element_type=F32)


def _dot_nt(a, b):
    return lax.dot_general(a, b, (((1,), (1,)), ((), ())), preferred_element_type=F32)


def _dot_tn(a, b):
    return lax.dot_general(a, b, (((0,), (0,)), ((), ())), preferred_element_type=F32)


def _split_bf16(x):
    hi = x.astype(BF16)
    lo = (x - hi.astype(F32)).astype(BF16)
    return hi, lo


def _rms(x):
    return x * lax.rsqrt(jnp.mean(x * x, axis=-1, keepdims=True) + EPS)


def _silu(x):
    return x * (1.0 / (1.0 + jnp.exp(-x)))


def _params(*sem):
    return pltpu.CompilerParams(dimension_semantics=sem, vmem_limit_bytes=VMEM_LIMIT)


def _mod_kernel(cond_ref, w_ref, b_ref, out_ref):
    a_hi, a_lo = _split_bf16(_silu(cond_ref[...]))
    w_hi, w_lo = _split_bf16(w_ref[...])
    out_ref[...] = _dot(a_hi, w_hi) + _dot(a_hi, w_lo) + _dot(a_lo, w_hi) + b_ref[...]


def _modulation(cond, w_ada, b_ada):
    n = w_ada.shape[1]
    tn = 1536
    return pl.pallas_call(
        _mod_kernel,
        grid=(n // tn,),
        in_specs=[
            pl.BlockSpec((MOD_ROWS, D_MODEL), lambda j: (0, 0)),
            pl.BlockSpec((D_MODEL, tn), lambda j: (0, j)),
            pl.BlockSpec((1, tn), lambda j: (0, j)),
        ],
        out_specs=pl.BlockSpec((MOD_ROWS, tn), lambda j: (0, j)),
        out_shape=jax.ShapeDtypeStruct((MOD_ROWS, n), F32),
        compiler_params=_params("arbitrary"),
        name="modulation",
    )(cond, w_ada, b_ada)


def _rotate_pairs(x):
    lane = lax.broadcasted_iota(jnp.int32, x.shape, 1)
    return jnp.where((lane & 31) < 16, pltpu.roll(x, LANES - 16, 1), pltpu.roll(x, 16, 1))


def _proj_kernel(x_ref, sh_ref, sc_ref, gpre_ref, w_ref, gq_ref, gk_ref, bd_ref, *rest, rope):
    if rope:
        cos_ref, sin_ref, q_ref, k_ref, vt_ref = rest
    else:
        q_ref, k_ref, vt_ref, ka_ref, va_ref, kb_ref, vb_ref = rest
    h = _rms(x_ref[...]) * gpre_ref[...]
    h = h * (1.0 + sc_ref[0]) + sh_ref[0]
    p = _dot(h.astype(BF16), w_ref[...])
    bd = bd_ref[...]

    def head_norm(t, g):
        ss = _dot((t * t).astype(BF16), bd[: t.shape[1], : t.shape[1]])
        return t * lax.rsqrt(ss * (1.0 / HEAD_DIM) + EPS) * g

    gq = gq_ref[...]
    qa = [head_norm(p[:, c:c + 2 * LANES], gq[:, c:c + 2 * LANES]) for c in (0, 2 * LANES)]
    qa = [t[:, c:c + LANES] for t in qa for c in (0, LANES)]
    qb = [p[:, Q_W + c:Q_W + c + LANES] for c in range(0, Q_W, LANES)]
    k0 = 2 * Q_W
    ka = head_norm(p[:, k0:k0 + KV_W], gk_ref[...])
    va = p[:, k0 + KV_W:k0 + 2 * KV_W]
    kb = p[:, k0 + 2 * KV_W:k0 + 3 * KV_W]
    vb = p[:, k0 + 3 * KV_W:k0 + 4 * KV_W]
    if rope:
        cos = cos_ref[...]
        sin = sin_ref[...]
        rot = lambda t: t * cos + _rotate_pairs(t) * sin
        ka_r, kb_r = rot(ka), rot(kb)
        qa = [rot(t) for t in qa]
        qb = [rot(t) for t in qb]
    else:
        ka_r, kb_r = ka, kb
    scale = LOG2E * HEAD_DIM ** -0.5
    q_ref[...] = jnp.concatenate([t * scale for t in qa + qb], axis=1).astype(BF16)
    k_ref[...] = jnp.concatenate([ka_r, kb_r], axis=1).astype(BF16)
    v_t = jnp.concatenate([va, vb], axis=1).T
    vt_ref[0] = v_t.astype(BF16)
    if not rope:
        k_t = jnp.concatenate([ka, kb], axis=1).T
        ka_ref[0] = k_t[:KV_W]
        kb_ref[0] = k_t[KV_W:]
        va_ref[0] = v_t[:KV_W]
        vb_ref[0] = v_t[KV_W:]


def _projection(x2d, mod3, mod_row, rows_per_batch, tm, w_in_p, gpre, gq, gk, bd, rope_tabs):
    n_rows = x2d.shape[0]
    tiles_per_batch = rows_per_batch // tm
    rope = rope_tabs is not None

    def mod_spec(j):
        return pl.BlockSpec(
            (1, 1, D_MODEL), lambda i: (mod_row(i // tiles_per_batch) * N_MOD + j, 0, 0))

    const = lambda shape: pl.BlockSpec(shape, lambda i: (0, 0))
    in_specs = [
        pl.BlockSpec((tm, D_MODEL), lambda i: (i, 0)),
        mod_spec(0), mod_spec(1),
        const((1, D_MODEL)), const((D_MODEL, IN_W)), const((1, Q_W)), const((1, KV_W)),
        const((2 * LANES, 2 * LANES)),
    ]
    args = [x2d, mod3, mod3, gpre, w_in_p, gq, gk, bd]
    out_specs = [pl.BlockSpec((tm, 2 * Q_W), lambda i: (i, 0)),
                 pl.BlockSpec((tm, 2 * KV_W), lambda i: (i, 0)),
                 pl.BlockSpec((1, 2 * KV_W, tm),
                              lambda i: (i // tiles_per_batch, 0, i % tiles_per_batch))]
    out_shape = [jax.ShapeDtypeStruct((n_rows, 2 * Q_W), BF16),
                 jax.ShapeDtypeStruct((n_rows, 2 * KV_W), BF16),
                 jax.ShapeDtypeStruct((n_rows // rows_per_batch, 2 * KV_W, rows_per_batch), BF16)]
    if rope:
        tab_spec = pl.BlockSpec((tm, LANES), lambda i: (i % tiles_per_batch, 0))
        in_specs += [tab_spec, tab_spec]
        args += list(rope_tabs)
    else:
        out_specs += [pl.BlockSpec(
            (1, KV_W, tm), lambda i: (i // tiles_per_batch, 0, i % tiles_per_batch))] * 4
        out_shape += [jax.ShapeDtypeStruct(
            (n_rows // rows_per_batch, KV_W, rows_per_batch), F32)] * 4
    return pl.pallas_call(
        functools.partial(_proj_kernel, rope=rope),
        grid=(n_rows // tm,),
        in_specs=in_specs,
        out_specs=out_specs,
        out_shape=out_shape,
        compiler_params=_params("parallel"),
        name="projection_latent" if rope else "projection_context",
    )(*args)


def _band_keys(tq):
    assert WINDOW % LANES == 0 and tq % LANES == 0
    return tq + 2 * WINDOW


def _pair_scores(chunk, keys, masks, s_ref):
    lane = lax.broadcasted_iota(jnp.int32, chunk.shape, 1)
    zero = jnp.zeros_like(chunk)
    qpair = jnp.concatenate(
        [jnp.where(lane < HEAD_DIM, chunk, zero), jnp.where(lane >= HEAD_DIM, chunk, zero)], axis=0)
    off = 0
    for k, m in zip(keys, masks):
        s = _dot_nt(k, qpair)
        s_ref[off:off + k.shape[0], :] = s if m is None else jnp.where(m, s, NEG_INF)
        off += k.shape[0]


def _pair_softmax(s_ref, sink_row, e_ref):
    s = s_ref[...]
    mx = jnp.max(s, axis=0, keepdims=True)
    if sink_row is not None:
        mx = jnp.maximum(mx, sink_row)
    e = jnp.exp2(s - mx)
    denom = jnp.sum(e, axis=0, keepdims=True)
    if sink_row is not None:
        denom = denom + jnp.exp2(sink_row - mx)
    e_ref[...] = e.astype(BF16)
    return denom


def _pair_output(e_ref, values_t, denom, tq):
    off, o_t = 0, None
    for vt in values_t:
        part = _dot(vt, e_ref[off:off + vt.shape[1], :])
        o_t = part if o_t is None else o_t + part
        off += vt.shape[1]
    o_t = o_t * (1.0 / denom)
    feat = lax.broadcasted_iota(jnp.int32, (LANES, tq), 0)
    return jnp.where(feat < HEAD_DIM, o_t[:, :tq], o_t[:, tq:]).T


def _attn_kernel(sink_ref, q_ref, k_ref, vt_ref, x_ref, gt1_ref, sh2_ref, sc2_ref, gpost_ref,
                 gpre2_ref, wout_ref, wr_hi_ref, wr_lo_ref, *rest, latent, tq):
    n_pairs = Q_W // LANES
    n_chains = 2 * n_pairs
    s_refs, e_refs = rest[-2 * n_chains:-n_chains], rest[-n_chains:]
    rest = rest[:-2 * n_chains]
    if latent:
        cak_ref, cav_ref, cbk_ref, cbv_ref, x1_ref, h2_ref, afft_ref = rest
    else:
        x1_ref, h2_ref, afft_ref = rest
    col = lax.broadcasted_iota(jnp.int32, (1, 2 * tq), 1)
    if latent:
        n_tok = k_ref.shape[0]
        band = _band_keys(tq)
        t0 = pl.program_id(1) * tq
        start = pl.multiple_of(jnp.clip(t0 - WINDOW, 0, n_tok - band), LANES)
        keys_a = [cak_ref[0].T.astype(BF16), k_ref[:, :KV_W]]
        vals_a = [cav_ref[0].astype(BF16), vt_ref[0, :KV_W, :]]
        keys_b = [cbk_ref[0].T.astype(BF16), k_ref[pl.ds(start, band), KV_W:]]
        vals_b = [cbv_ref[0].astype(BF16), vt_ref[0, KV_W:, pl.ds(start, band)]]
        kpos = start + lax.broadcasted_iota(jnp.int32, (band, 2 * tq), 0)
        qpos = t0 + (lax.broadcasted_iota(jnp.int32, (band, 2 * tq), 1) & (tq - 1))
        masks_a, masks_b = [None, None], [None, jnp.abs(kpos - qpos) <= WINDOW]
    else:
        keys_a, vals_a = [k_ref[:, :KV_W]], [vt_ref[0, :KV_W, :]]
        keys_b, vals_b = [k_ref[:, KV_W:]], [vt_ref[0, KV_W:, :]]
        masks_a, masks_b = [None], [None]
    def chain(c):
        if c < n_pairs:
            return keys_a, vals_a, masks_a, None
        j = c - n_pairs
        sink_row = jnp.where(col < tq, sink_ref[j], sink_ref[j + n_pairs]) * LOG2E
        return keys_b, vals_b, masks_b, sink_row

    def scores(c):
        keys, _, masks, _ = chain(c)
        _pair_scores(q_ref[:, c * LANES:(c + 1) * LANES], keys, masks, s_refs[c])

    order = [c for j in range(n_pairs) for c in (j, j + n_pairs)]
    chunks = [None] * n_chains
    scores(order[0])
    for i, c in enumerate(order):
        if i + 1 < n_chains:
            scores(order[i + 1])
        _, vals, _, sink_row = chain(c)
        denom = _pair_softmax(s_refs[c], sink_row, e_refs[c])
        chunks[c] = _pair_output(e_refs[c], vals, denom, tq)
    o = jnp.concatenate(chunks, axis=1)
    proj = _dot(o.astype(BF16), wout_ref[...])
    x1 = x_ref[...] + gt1_ref[0] * (_rms(proj) * gpost_ref[...])
    x1_ref[...] = x1
    h2 = (_rms(x1) * gpre2_ref[...]) * (1.0 + sc2_ref[0]) + sh2_ref[0]
    h2_ref[...] = h2.astype(BF16)
    h_hi, h_lo = _split_bf16(h2)
    wr_hi = wr_hi_ref[...]
    logits = _dot(h_hi, wr_hi) + _dot(h_lo, wr_hi) + _dot(h_hi, wr_lo_ref[...])
    lane = lax.broadcasted_iota(jnp.int32, logits.shape, 1)
    logits = jnp.where(lane < N_EXPERTS, logits, NEG_INF)
    ex = jnp.exp(logits - jnp.max(logits, axis=-1, keepdims=True))
    aff = ex * (1.0 / jnp.sum(ex, axis=-1, keepdims=True))
    afft_ref[0] = aff.T[:N_EXPERTS]


def _attention(sink_p, q, k, vt, x2d, mod3, mod_row, n_batch, n_tok, tq, gpost, gpre2, wout_p,
               wr_hi, wr_lo, caches):
    latent = caches is not None
    nq = n_tok // tq
    row = lambda b, i: (b * nq + i, 0)

    def mod_spec(j):
        return pl.BlockSpec((1, 1, D_MODEL), lambda b, i: (mod_row(b) * N_MOD + j, 0, 0))

    const = lambda shape: pl.BlockSpec(shape, lambda b, i: (0, 0))
    in_specs = [
        pl.BlockSpec(memory_space=pltpu.SMEM),
        pl.BlockSpec((tq, 2 * Q_W), row),
        pl.BlockSpec((n_tok, 2 * KV_W), lambda b, i: (b, 0)),
        pl.BlockSpec((1, 2 * KV_W, n_tok), lambda b, i: (b, 0, 0)),
        pl.BlockSpec((tq, D_MODEL), row),
        mod_spec(2), mod_spec(3), mod_spec(4),
        const((1, D_MODEL)), const((1, D_MODEL)),
        const((2 * Q_W, D_MODEL)), const((D_MODEL, LANES)), const((D_MODEL, LANES)),
    ]
    args = [sink_p, q, k, vt, x2d, mod3, mod3, mod3, gpost, gpre2, wout_p, wr_hi, wr_lo]
    if latent:
        cache_spec = pl.BlockSpec((1,) + caches[0].shape[1:], lambda b, i: (b, 0, 0))
        in_specs += [cache_spec] * 4
        args += list(caches)
    out_specs = [
        pl.BlockSpec((tq, D_MODEL), row),
        pl.BlockSpec((tq, D_MODEL), row),
        pl.BlockSpec((1, N_EXPERTS, tq), lambda b, i: (b, 0, i)),
    ]
    out_shape = [
        jax.ShapeDtypeStruct((n_batch * n_tok, D_MODEL), F32),
        jax.ShapeDtypeStruct((n_batch * n_tok, D_MODEL), BF16),
        jax.ShapeDtypeStruct((n_batch, N_EXPERTS, n_tok), F32),
    ]
    n_ctx_keys = caches[0].shape[2] if latent else 0
    keys_a = n_ctx_keys + n_tok
    keys_b = n_ctx_keys + (_band_keys(tq) if latent else n_tok)
    n_pairs = Q_W // LANES
    key_counts = [keys_a] * n_pairs + [keys_b] * n_pairs
    scratch = ([pltpu.VMEM((n, 2 * tq), F32) for n in key_counts]
               + [pltpu.VMEM((n, 2 * tq), BF16) for n in key_counts])
    return pl.pallas_call(
        functools.partial(_attn_kernel, latent=latent, tq=tq),
        grid=(n_batch, nq),
        in_specs=in_specs,
        out_specs=out_specs,
        out_shape=out_shape,
        scratch_shapes=scratch,
        compiler_params=_params("parallel", "parallel"),
        name="attention_latent" if latent else "attention_context",
    )(*args)


def _route_kernel(afft_ref, tri_ref, slot_ref, *, cap):
    a = afft_ref[...]
    capf = float(cap)
    count_ge = lambda t: jnp.sum(jnp.where(a >= t, 1.0, 0.0), axis=1, keepdims=True)
    p = jnp.full((a.shape[0], 1), 2.0, F32)
    for step in EXP_STEPS:
        cand = p * (2.0 ** -step)
        p = jnp.where(count_ge(cand) < capf, cand, p)
    lo0 = p * 0.5

    def refine(_, carry):
        lo, delta = carry
        cand = lo + delta
        return jnp.where(count_ge(cand) >= capf, cand, lo), delta * 0.5

    thr, _ = lax.fori_loop(0, MANTISSA_STEPS, refine, (lo0, lo0 * 0.5))
    above = a > thr
    tied = a == thr
    need = capf - jnp.sum(jnp.where(above, 1.0, 0.0), axis=1, keepdims=True)
    tri = tri_ref[...]
    tie_rank = _dot(jnp.where(tied, 1.0, 0.0).astype(BF16), tri)
    sel = jnp.logical_or(above, jnp.logical_and(tied, tie_rank < need))
    slot = _dot(jnp.where(sel, 1.0, 0.0).astype(BF16), tri)
    slot_ref[...] = jnp.where(sel, slot, -1.0).astype(jnp.int32)


def _route(afft2d, tri, cap):
    rows, n_tok = afft2d.shape
    return pl.pallas_call(
        functools.partial(_route_kernel, cap=cap),
        grid=(1,),
        in_specs=[pl.BlockSpec((rows, n_tok), lambda i: (0, 0)),
                  pl.BlockSpec((n_tok, n_tok), lambda i: (0, 0))],
        out_specs=pl.BlockSpec((rows, n_tok), lambda i: (0, 0)),
        out_shape=jax.ShapeDtypeStruct((rows, n_tok), jnp.int32),
        compiler_params=_params("arbitrary"),
        name="route",
    )(afft2d, tri)


def _one_hot_rows(slot_row, cap):
    return slot_row == lax.broadcasted_iota(jnp.int32, (cap, slot_row.shape[1]), 0)


def _gather_kernel(slot_ref, afft_ref, h_ref, xg_ref, g_ref, *, cap):
    h = h_ref[...]
    onehots = []
    for e in range(N_EXPERTS):
        onehot = _one_hot_rows(slot_ref[e:e + 1, :], cap)
        onehots.append(jnp.where(onehot, 1.0, 0.0).astype(BF16))
        g = jnp.sum(jnp.where(onehot, afft_ref[0, e:e + 1, :], 0.0), axis=1, keepdims=True)
        g_ref[e] = jnp.broadcast_to(g, (cap, LANES))
    x = _dot(jnp.concatenate(onehots, axis=0), h)
    xg_ref[...] = x.astype(BF16).reshape(N_EXPERTS, cap, D_MODEL)


def _gather(slot2d, afft, h2, n_batch, n_tok, cap):
    return pl.pallas_call(
        functools.partial(_gather_kernel, cap=cap),
        grid=(n_batch,),
        in_specs=[
            pl.BlockSpec((N_EXPERTS, n_tok), lambda b: (b, 0)),
            pl.BlockSpec((1, N_EXPERTS, n_tok), lambda b: (b, 0, 0)),
            pl.BlockSpec((n_tok, D_MODEL), lambda b: (b, 0)),
        ],
        out_specs=[
            pl.BlockSpec((N_EXPERTS, cap, D_MODEL), lambda b: (0, b, 0)),
            pl.BlockSpec((N_EXPERTS, cap, LANES), lambda b: (0, b, 0)),
        ],
        out_shape=[
            jax.ShapeDtypeStruct((N_EXPERTS, n_batch * cap, D_MODEL), BF16),
            jax.ShapeDtypeStruct((N_EXPERTS, n_batch * cap, LANES), F32),
        ],
        compiler_params=_params("parallel"),
        name="gather",
    )(slot2d, afft, h2)


def _ffn_kernel(xc_ref, xl_ref, gc_ref, gl_ref, wg_ref, wu_ref, wd_ref, yc_ref, yl_ref,
                acc_c, acc_l):
    f = pl.program_id(1)
    wg = wg_ref[0].astype(BF16)
    wu = wu_ref[0].astype(BF16)
    wd = wd_ref[0].astype(BF16)

    @pl.when(f == 0)
    def _():
        acc_c[...] = jnp.zeros_like(acc_c)
        acc_l[...] = jnp.zeros_like(acc_l)

    for x_ref, acc in ((xc_ref, acc_c), (xl_ref, acc_l)):
        x = x_ref[0]
        act = (_silu(_dot(x, wg)) * _dot(x, wu)).astype(BF16)
        acc[...] += _dot(act, wd)

    @pl.when(f == pl.num_programs(1) - 1)
    def _():
        for acc, g_ref, y_ref in ((acc_c, gc_ref, yc_ref), (acc_l, gl_ref, yl_ref)):
            y_ref[0] = (acc[...] * jnp.tile(g_ref[0], (1, D_MODEL // LANES))).astype(BF16)


def _ffn(xc, xl, gc, gl, w_gate, w_up, w_down):
    rows_c, rows_l = xc.shape[1], xl.shape[1]
    d_ff = w_gate.shape[2]
    fc = FFN_HIDDEN_CHUNK
    row_map = lambda e, f: (e, 0, 0)
    return pl.pallas_call(
        _ffn_kernel,
        grid=(N_EXPERTS, d_ff // fc),
        in_specs=[
            pl.BlockSpec((1, rows_c, D_MODEL), row_map),
            pl.BlockSpec((1, rows_l, D_MODEL), row_map),
            pl.BlockSpec((1, rows_c, LANES), row_map),
            pl.BlockSpec((1, rows_l, LANES), row_map),
            pl.BlockSpec((1, D_MODEL, fc), lambda e, f: (e, 0, f)),
            pl.BlockSpec((1, D_MODEL, fc), lambda e, f: (e, 0, f)),
            pl.BlockSpec((1, fc, D_MODEL), lambda e, f: (e, f, 0)),
        ],
        out_specs=[
            pl.BlockSpec((1, rows_c, D_MODEL), row_map),
            pl.BlockSpec((1, rows_l, D_MODEL), row_map),
        ],
        out_shape=[jax.ShapeDtypeStruct(xc.shape, BF16), jax.ShapeDtypeStruct(xl.shape, BF16)],
        scratch_shapes=[pltpu.VMEM((rows_c, D_MODEL), F32), pltpu.VMEM((rows_l, D_MODEL), F32)],
        compiler_params=_params("arbitrary", "arbitrary"),
        name="expert_ffn",
    )(xc, xl, gc, gl, w_gate, w_up, w_down)


def _scatter_kernel(slot_ref, y_ref, x1_ref, gt2_ref, gpost_ref, out_ref, *, cap):
    onehot = jnp.concatenate(
        [jnp.where(_one_hot_rows(slot_ref[e:e + 1, :], cap), 1.0, 0.0).astype(BF16)
         for e in range(N_EXPERTS)], axis=0)
    y = y_ref[...].reshape(N_EXPERTS * cap, D_MODEL)
    moe = _dot_tn(onehot, y)
    out_ref[...] = x1_ref[...] + gt2_ref[0] * (_rms(moe) * gpost_ref[...])


def _scatter(slot2d, y, x1, mod3, mod_row, n_batch, n_tok, cap, gpost):
    return pl.pallas_call(
        functools.partial(_scatter_kernel, cap=cap),
        grid=(n_batch,),
        in_specs=[
            pl.BlockSpec((N_EXPERTS, n_tok), lambda b: (b, 0)),
            pl.BlockSpec((N_EXPERTS, cap, D_MODEL), lambda b: (0, b, 0)),
            pl.BlockSpec((n_tok, D_MODEL), lambda b: (b, 0)),
            pl.BlockSpec((1, 1, D_MODEL), lambda b: (mod_row(b) * N_MOD + 5, 0, 0)),
            pl.BlockSpec((1, D_MODEL), lambda b: (0, 0)),
        ],
        out_specs=pl.BlockSpec((n_tok, D_MODEL), lambda b: (b, 0)),
        out_shape=jax.ShapeDtypeStruct((n_batch * n_tok, D_MODEL), F32),
        compiler_params=_params("parallel"),
        name="scatter",
    )(slot2d, y, x1, mod3, gpost)


def _pair_heads(w, axis):
    shape = w.shape
    w = w.reshape(shape[:axis] + (2, 4, HEAD_DIM) + shape[axis + 1:])
    w = jnp.swapaxes(w, axis, axis + 1)
    return w.reshape(shape)


def _rope_tables(n_tok):
    f32 = np.float32
    half = HEAD_DIM // 4
    freqs = f32(ROPE_THETA) ** (-np.arange(half, dtype=f32) / f32(half))
    rows = np.repeat(np.arange(n_tok // GRID_W, dtype=f32), GRID_W)
    cols = np.tile(np.arange(GRID_W, dtype=f32), n_tok // GRID_W)
    d = np.arange(HEAD_DIM)
    pos = np.where((d // (2 * half))[None, :] == 0, rows[:, None], cols[:, None])
    ang = (pos * freqs[d % half][None, :]).astype(f32)
    sign = np.where((d % (2 * half)) < half, f32(-1.0), f32(1.0))
    cos = np.tile(np.cos(ang), (1, LANES // HEAD_DIM))
    sin = np.tile(np.sin(ang) * sign[None, :], (1, LANES // HEAD_DIM))
    return cos.astype(f32), sin.astype(f32)


def _prefix_matrix(n):
    r = np.arange(n)
    return (r[:, None] < r[None, :]).astype(BF16)


def kernel(x_prompt, x_sample, cache_a_k, cache_a_v, cache_b_k, cache_b_v, c, c_ctx, w_ada, b_ada,
           g_pre_attn, w_in, g_qa, g_ka, sink_b, w_out, g_post_attn, g_pre_ffn, w_router, w_gate,
           w_up, w_down, g_post_ffn):
    n_ctx, t_ctx, _ = x_prompt.shape
    n_lat, t_lat, _ = x_sample.shape
    assert w_ada.shape[0] == 1, "single-layer step"
    assert n_lat + 1 <= MOD_ROWS

    w_in0 = w_in[0]
    c1, c2, c3, c4, c5 = Q_W, Q_W + KV_W, Q_W + 2 * KV_W, 2 * Q_W + 2 * KV_W, 2 * Q_W + 3 * KV_W
    w_in_p = jnp.concatenate(
        [_pair_heads(w_in0[:, :c1], 1), _pair_heads(w_in0[:, c3:c4], 1),
         w_in0[:, c1:c3], w_in0[:, c4:]], axis=1).astype(BF16)
    w_out0 = w_out[0]
    w_out_p = jnp.concatenate(
        [_pair_heads(w_out0[:Q_W], 0), _pair_heads(w_out0[Q_W:], 0)], axis=0).astype(BF16)
    wr = jnp.pad(w_router[0], ((0, 0), (0, LANES - N_EXPERTS)))
    wr_hi = wr.astype(BF16)
    wr_lo = (wr - wr_hi.astype(F32)).astype(BF16)
    gq = jnp.tile(g_qa[0], N_HEADS)[None, :]
    gk = jnp.tile(g_ka[0], N_KV)[None, :]
    blk = np.arange(2 * LANES) // HEAD_DIM
    bd = (blk[:, None] == blk[None, :]).astype(BF16)
    sink = sink_b[0].astype(F32)
    cond = jnp.concatenate(
        [c, c_ctx[None, :], jnp.zeros((MOD_ROWS - n_lat - 1, D_MODEL), F32)], axis=0)

    mod = _modulation(cond, w_ada[0], b_ada[0][None, :])
    mod3 = mod.reshape(MOD_ROWS * N_MOD, 1, D_MODEL)

    xc = x_prompt.reshape(n_ctx * t_ctx, D_MODEL)
    xl = x_sample.reshape(n_lat * t_lat, D_MODEL)
    gpre1, gpost1 = g_pre_attn[0][None, :], g_post_attn[0][None, :]
    gpre2, gpost2 = g_pre_ffn[0][None, :], g_post_ffn[0][None, :]

    ctx_row = lambda b: n_lat
    lat_row = lambda b: b

    qc, kc, vtc, ka, va, kb, vb = _projection(
        xc, mod3, ctx_row, t_ctx, t_ctx, w_in_p, gpre1, gq, gk, bd, None)
    x1c, h2c, afftc = _attention(
        sink, qc, kc, vtc, xc, mod3, ctx_row, n_ctx, t_ctx, t_ctx, gpost1, gpre2, w_out_p,
        wr_hi, wr_lo, None)

    ql, kl, vtl = _projection(
        xl, mod3, lat_row, t_lat, LATENT_PROJ_TILE, w_in_p, gpre1, gq, gk, bd, _rope_tables(t_lat))
    caches = [jnp.transpose(t[:, 0], (0, 2, 3, 1)).reshape(n_lat, KV_W, t.shape[2])
              for t in (cache_a_k, cache_a_v, cache_b_k, cache_b_v)]
    x1l, h2l, afftl = _attention(
        sink, ql, kl, vtl, xl, mod3, lat_row, n_lat, t_lat, LATENT_Q_TILE, gpost1, gpre2, w_out_p,
        wr_hi, wr_lo, caches)

    cap_c = CAPACITY_FACTOR * t_ctx // N_EXPERTS
    cap_l = CAPACITY_FACTOR * t_lat // N_EXPERTS
    slot_c = _route(afftc.reshape(n_ctx * N_EXPERTS, t_ctx), _prefix_matrix(t_ctx), cap_c)
    slot_l = _route(afftl.reshape(n_lat * N_EXPERTS, t_lat), _prefix_matrix(t_lat), cap_l)
    xgc, gc = _gather(slot_c, afftc, h2c, n_ctx, t_ctx, cap_c)
    xgl, gl = _gather(slot_l, afftl, h2l, n_lat, t_lat, cap_l)
    yc, yl = _ffn(xgc, xgl, gc, gl, w_gate[0], w_up[0], w_down[0])
    y_prompt = _scatter(slot_c, yc, x1c, mod3, ctx_row, n_ctx, t_ctx, cap_c, gpost2)
    y_sample = _scatter(slot_l, yl, x1l, mod3, lat_row, n_lat, t_lat, cap_l, gpost2)

    new_cache = lambda t: jnp.transpose(
        t.reshape(n_ctx, 1, N_KV, HEAD_DIM, t_ctx), (0, 1, 4, 2, 3))
    return (y_prompt.reshape(x_prompt.shape), y_sample.reshape(x_sample.shape),
            new_cache(ka), new_cache(va), new_cache(kb), new_cache(vb))
```

```python
import functools

import jax
import jax.numpy as jnp
import numpy as np
from jax import lax
from jax.experimental import pallas as pl
from jax.experimental.pallas import tpu as pltpu

F32 = jnp.float32
BF16 = jnp.bfloat16

D_MODEL = 1024
HEAD_DIM = 64
N_HEADS = 8
N_KV = 2
Q_W = N_HEADS * HEAD_DIM
KV_W = N_KV * HEAD_DIM
IN_W = 2 * (Q_W + 2 * KV_W)
GRID_W = 64
BLOCK = 128
WINDOW = 128
ROPE_THETA = 10000.0
N_EXPERTS = 16
CAPACITY_FACTOR = 2
N_MOD = 6
EPS = 1e-6
NEG_INF = -1e30
LOG2E = 1.4426950408889634
LANES = 128
MOD_ROWS = 16
VMEM_LIMIT = 56 * 1024 * 1024
FFN_ROW_TILE = 512
LATENT_Q_TILE = 256
LATENT_PROJ_TILE = 512
SCORE_LOOKAHEAD = 4

EXP_STEPS = (64, 32, 16, 8, 4, 2, 1)
MANTISSA_STEPS = 36


def _dot(a, b):
    return jnp.dot(a, b, preferred_element_type=F32)


def _dot_nt(a, b):
    return lax.dot_general(a, b, (((1,), (1,)), ((), ())), preferred_element_type=F32)


def _dot_tn(a, b):
    return lax.dot_general(a, b, (((0,), (0,)), ((), ())), preferred_element_type=F32)


def _split_bf16(x):
    hi = x.astype(BF16)
    lo = (x - hi.astype(F32)).astype(BF16)
    return hi, lo


def _rms(x):
    return x * lax.rsqrt(jnp.mean(x * x, axis=-1, keepdims=True) + EPS)


def _silu(x):
    return x * (1.0 / (1.0 + jnp.exp(-x)))


def _params(*sem):
    return pltpu.CompilerParams(dimension_semantics=sem, vmem_limit_bytes=VMEM_LIMIT)


def _mod_kernel(cond_ref, w_ref, b_ref, out_ref):
    a_hi, a_lo = _split_bf16(_silu(cond_ref[...]))
    w_hi, w_lo = _split_bf16(w_ref[...])
    out_ref[...] = _dot(a_hi, w_hi) + _dot(a_hi, w_lo) + _dot(a_lo, w_hi) + b_ref[...]


def _modulation(cond, w_ada, b_ada):
    n = w_ada.shape[1]
    tn = 1536
    return pl.pallas_call(
        _mod_kernel,
        grid=(n // tn,),
        in_specs=[
            pl.BlockSpec((MOD_ROWS, D_MODEL), lambda j: (0, 0)),
            pl.BlockSpec((D_MODEL, tn), lambda j: (0, j)),
            pl.BlockSpec((1, tn), lambda j: (0, j)),
        ],
        out_specs=pl.BlockSpec((MOD_ROWS, tn), lambda j: (0, j)),
        out_shape=jax.ShapeDtypeStruct((MOD_ROWS, n), F32),
        compiler_params=_params("arbitrary"),
        name="modulation",
    )(cond, w_ada, b_ada)


def _rotate_pairs(x):
    lane = lax.broadcasted_iota(jnp.int32, x.shape, 1)
    return jnp.where((lane & 31) < 16, pltpu.roll(x, LANES - 16, 1), pltpu.roll(x, 16, 1))


def _proj_kernel(x_ref, sh_ref, sc_ref, gpre_ref, w_ref, gq_ref, gk_ref, bd_ref, *rest, rope):
    if rope:
        cos_ref, sin_ref, q_ref, k_ref, vt_ref = rest
    else:
        q_ref, k_ref, vt_ref, ka_ref, va_ref, kb_ref, vb_ref = rest
    h = _rms(x_ref[...]) * gpre_ref[...]
    h = h * (1.0 + sc_ref[0]) + sh_ref[0]
    p = _dot(h.astype(BF16), w_ref[...])
    bd = bd_ref[...]

    def head_norm(t, g):
        ss = _dot((t * t).astype(BF16), bd[: t.shape[1], : t.shape[1]])
        return t * lax.rsqrt(ss * (1.0 / HEAD_DIM) + EPS) * g

    gq = gq_ref[...]
    qa = [head_norm(p[:, c:c + 2 * LANES], gq[:, c:c + 2 * LANES]) for c in (0, 2 * LANES)]
    qa = [t[:, c:c + LANES] for t in qa for c in (0, LANES)]
    qb = [p[:, Q_W + c:Q_W + c + LANES] for c in range(0, Q_W, LANES)]
    k0 = 2 * Q_W
    ka = head_norm(p[:, k0:k0 + KV_W], gk_ref[...])
    va = p[:, k0 + KV_W:k0 + 2 * KV_W]
    kb = p[:, k0 + 2 * KV_W:k0 + 3 * KV_W]
    vb = p[:, k0 + 3 * KV_W:k0 + 4 * KV_W]
    if rope:
        cos = cos_ref[...]
        sin = sin_ref[...]
        rot = lambda t: t * cos + _rotate_pairs(t) * sin
        ka_r, kb_r = rot(ka), rot(kb)
        qa = [rot(t) for t in qa]
        qb = [rot(t) for t in qb]
    else:
        ka_r, kb_r = ka, kb
    scale = LOG2E * HEAD_DIM ** -0.5
    q_ref[...] = jnp.concatenate([t * scale for t in qa + qb], axis=1).astype(BF16)
    k_ref[...] = jnp.concatenate([ka_r, kb_r], axis=1).astype(BF16)
    v_t = jnp.concatenate([va, vb], axis=1).T
    vt_ref[0] = v_t.astype(BF16)
    if not rope:
        k_t = jnp.concatenate([ka, kb], axis=1).T
        ka_ref[0] = k_t[:KV_W]
        kb_ref[0] = k_t[KV_W:]
        va_ref[0] = v_t[:KV_W]
        vb_ref[0] = v_t[KV_W:]


def _projection(x2d, mod3, mod_row, rows_per_batch, tm, w_in_p, gpre, gq, gk, bd, rope_tabs):
    n_rows = x2d.shape[0]
    tiles_per_batch = rows_per_batch // tm
    rope = rope_tabs is not None

    def mod_spec(j):
        return pl.BlockSpec(
            (1, 1, D_MODEL), lambda i: (mod_row(i // tiles_per_batch) * N_MOD + j, 0, 0))

    const = lambda shape: pl.BlockSpec(shape, lambda i: (0, 0))
    in_specs = [
        pl.BlockSpec((tm, D_MODEL), lambda i: (i, 0)),
        mod_spec(0), mod_spec(1),
        const((1, D_MODEL)), const((D_MODEL, IN_W)), const((1, Q_W)), const((1, KV_W)),
        const((2 * LANES, 2 * LANES)),
    ]
    args = [x2d, mod3, mod3, gpre, w_in_p, gq, gk, bd]
    out_specs = [pl.BlockSpec((tm, 2 * Q_W), lambda i: (i, 0)),
                 pl.BlockSpec((tm, 2 * KV_W), lambda i: (i, 0)),
                 pl.BlockSpec((1, 2 * KV_W, tm),
                              lambda i: (i // tiles_per_batch, 0, i % tiles_per_batch))]
    out_shape = [jax.ShapeDtypeStruct((n_rows, 2 * Q_W), BF16),
                 jax.ShapeDtypeStruct((n_rows, 2 * KV_W), BF16),
                 jax.ShapeDtypeStruct((n_rows // rows_per_batch, 2 * KV_W, rows_per_batch), BF16)]
    if rope:
        tab_spec = pl.BlockSpec((tm, LANES), lambda i: (i % tiles_per_batch, 0))
        in_specs += [tab_spec, tab_spec]
        args += list(rope_tabs)
    else:
        out_specs += [pl.BlockSpec(
            (1, KV_W, tm), lambda i: (i // tiles_per_batch, 0, i % tiles_per_batch))] * 4
        out_shape += [jax.ShapeDtypeStruct(
            (n_rows // rows_per_batch, KV_W, rows_per_batch), F32)] * 4
    return pl.pallas_call(
        functools.partial(_proj_kernel, rope=rope),
        grid=(n_rows // tm,),
        in_specs=in_specs,
        out_specs=out_specs,
        out_shape=out_shape,
        compiler_params=_params("parallel"),
        name="projection_latent" if rope else "projection_context",
    )(*args)


def _band_keys(tq):
    assert WINDOW % LANES == 0 and tq % LANES == 0
    return tq + 2 * WINDOW


def _pair_scores(chunk, keys, masks, sink_row, s_ref):
    lane = lax.broadcasted_iota(jnp.int32, chunk.shape, 1)
    zero = jnp.zeros_like(chunk)
    qpair = jnp.concatenate(
        [jnp.where(lane < HEAD_DIM, chunk, zero), jnp.where(lane >= HEAD_DIM, chunk, zero)], axis=0)
    off, mx = 0, sink_row
    for k, m in zip(keys, masks):
        s = _dot_nt(k, qpair)
        s = s if m is None else jnp.where(m, s, NEG_INF)
        s_ref[off:off + k.shape[0], :] = s
        part = jnp.max(s, axis=0, keepdims=True)
        mx = part if mx is None else jnp.maximum(mx, part)
        off += k.shape[0]
    return mx


def _pair_softmax(s_ref, mx, sink_row, e_ref):
    e = jnp.exp2(s_ref[...] - mx)
    denom = jnp.sum(e, axis=0, keepdims=True)
    if sink_row is not None:
        denom = denom + jnp.exp2(sink_row - mx)
    e_ref[...] = e.astype(BF16)
    return denom


def _pair_output(e_ref, values_t, denom, tq):
    off, o_t = 0, None
    for vt in values_t:
        part = _dot(vt, e_ref[off:off + vt.shape[1], :])
        o_t = part if o_t is None else o_t + part
        off += vt.shape[1]
    o_t = o_t * (1.0 / denom)
    feat = lax.broadcasted_iota(jnp.int32, (LANES, tq), 0)
    return jnp.where(feat < HEAD_DIM, o_t[:, :tq], o_t[:, tq:]).T


def _attn_kernel(sink_ref, q_ref, k_ref, vt_ref, x_ref, gt1_ref, sh2_ref, sc2_ref, gpost_ref,
                 gpre2_ref, wout_ref, wr_ref, *rest, latent, tq):
    n_pairs = Q_W // LANES
    n_chains = 2 * n_pairs
    s_refs, e_refs = rest[-2 * n_chains:-n_chains], rest[-n_chains:]
    rest = rest[:-2 * n_chains]
    if latent:
        cak_ref, cav_ref, cbk_ref, cbv_ref, x1_ref, h2_ref, afft_ref = rest
    else:
        x1_ref, h2_ref, afft_ref = rest
    col = lax.broadcasted_iota(jnp.int32, (1, 2 * tq), 1)
    if latent:
        n_tok = k_ref.shape[0]
        band = _band_keys(tq)
        t0 = pl.program_id(1) * tq
        start = pl.multiple_of(jnp.clip(t0 - WINDOW, 0, n_tok - band), LANES)
        keys_a = [cak_ref[0].T.astype(BF16), k_ref[:, :KV_W]]
        vals_a = [cav_ref[0].astype(BF16), vt_ref[0, :KV_W, :]]
        keys_b = [cbk_ref[0].T.astype(BF16), k_ref[pl.ds(start, band), KV_W:]]
        vals_b = [cbv_ref[0].astype(BF16), vt_ref[0, KV_W:, pl.ds(start, band)]]
        kpos = start + lax.broadcasted_iota(jnp.int32, (band, 2 * tq), 0)
        qpos = t0 + (lax.broadcasted_iota(jnp.int32, (band, 2 * tq), 1) & (tq - 1))
        masks_a, masks_b = [None, None], [None, jnp.abs(kpos - qpos) <= WINDOW]
    else:
        keys_a, vals_a = [k_ref[:, :KV_W]], [vt_ref[0, :KV_W, :]]
        keys_b, vals_b = [k_ref[:, KV_W:]], [vt_ref[0, KV_W:, :]]
        masks_a, masks_b = [None], [None]
    def chain(c):
        if c < n_pairs:
            return keys_a, vals_a, masks_a, None
        j = c - n_pairs
        sink_row = jnp.where(col < tq, sink_ref[j], sink_ref[j + n_pairs]) * LOG2E
        return keys_b, vals_b, masks_b, sink_row

    col_max = [None] * n_chains

    def scores(c):
        keys, _, masks, sink_row = chain(c)
        col_max[c] = _pair_scores(
            q_ref[:, c * LANES:(c + 1) * LANES], keys, masks, sink_row, s_refs[c])

    order = [c for j in range(n_pairs) for c in (j, j + n_pairs)]
    chunks = [None] * n_chains
    proj = None
    for c in order[:SCORE_LOOKAHEAD]:
        scores(c)
    for i, c in enumerate(order):
        if i + SCORE_LOOKAHEAD < n_chains:
            scores(order[i + SCORE_LOOKAHEAD])
        _, vals, _, sink_row = chain(c)
        denom = _pair_softmax(s_refs[c], col_max[c], sink_row, e_refs[c])
        chunks[c] = _pair_output(e_refs[c], vals, denom, tq)
        if c >= n_pairs:
            j = c - n_pairs
            o_pair = jnp.concatenate([chunks[j], chunks[c]], axis=1).astype(BF16)
            part = _dot(o_pair, wout_ref[2 * j * LANES:2 * (j + 1) * LANES, :])
            proj = part if proj is None else proj + part
    x1 = x_ref[...] + gt1_ref[0] * (_rms(proj) * gpost_ref[...])
    x1_ref[...] = x1
    h2 = (_rms(x1) * gpre2_ref[...]) * (1.0 + sc2_ref[0]) + sh2_ref[0]
    h2_ref[...] = h2.astype(BF16)
    h_hi, h_lo = _split_bf16(h2)
    wr = wr_ref[...]
    p_hi = _dot(h_hi, wr)
    logits = p_hi + pltpu.roll(p_hi, LANES - N_EXPERTS, 1) + _dot(h_lo, wr)
    lane = lax.broadcasted_iota(jnp.int32, logits.shape, 1)
    logits = jnp.where(lane < N_EXPERTS, logits, NEG_INF)
    ex = jnp.exp(logits - jnp.max(logits, axis=-1, keepdims=True))
    aff = ex * (1.0 / jnp.sum(ex, axis=-1, keepdims=True))
    afft_ref[0] = aff.T[:N_EXPERTS]


def _attention(sink_p, q, k, vt, x2d, mod3, mod_row, n_batch, n_tok, tq, gpost, gpre2, wout_p,
               wr_split, caches):
    latent = caches is not None
    nq = n_tok // tq
    row = lambda b, i: (b * nq + i, 0)

    def mod_spec(j):
        return pl.BlockSpec((1, 1, D_MODEL), lambda b, i: (mod_row(b) * N_MOD + j, 0, 0))

    const = lambda shape: pl.BlockSpec(shape, lambda b, i: (0, 0))
    in_specs = [
        pl.BlockSpec(memory_space=pltpu.SMEM),
        pl.BlockSpec((tq, 2 * Q_W), row),
        pl.BlockSpec((n_tok, 2 * KV_W), lambda b, i: (b, 0)),
        pl.BlockSpec((1, 2 * KV_W, n_tok), lambda b, i: (b, 0, 0)),
        pl.BlockSpec((tq, D_MODEL), row),
        mod_spec(2), mod_spec(3), mod_spec(4),
        const((1, D_MODEL)), const((1, D_MODEL)),
        const((2 * Q_W, D_MODEL)), const((D_MODEL, LANES)),
    ]
    args = [sink_p, q, k, vt, x2d, mod3, mod3, mod3, gpost, gpre2, wout_p, wr_split]
    if latent:
        cache_spec = pl.BlockSpec((1,) + caches[0].shape[1:], lambda b, i: (b, 0, 0))
        in_specs += [cache_spec] * 4
        args += list(caches)
    out_specs = [
        pl.BlockSpec((tq, D_MODEL), row),
        pl.BlockSpec((tq, D_MODEL), row),
        pl.BlockSpec((1, N_EXPERTS, tq), lambda b, i: (b, 0, i)),
    ]
    out_shape = [
        jax.ShapeDtypeStruct((n_batch * n_tok, D_MODEL), F32),
        jax.ShapeDtypeStruct((n_batch * n_tok, D_MODEL), BF16),
        jax.ShapeDtypeStruct((n_batch, N_EXPERTS, n_tok), F32),
    ]
    n_ctx_keys = caches[0].shape[2] if latent else 0
    keys_a = n_ctx_keys + n_tok
    keys_b = n_ctx_keys + (_band_keys(tq) if latent else n_tok)
    n_pairs = Q_W // LANES
    key_counts = [keys_a] * n_pairs + [keys_b] * n_pairs
    scratch = ([pltpu.VMEM((n, 2 * tq), F32) for n in key_counts]
               + [pltpu.VMEM((n, 2 * tq), BF16) for n in key_counts])
    return pl.pallas_call(
        functools.partial(_attn_kernel, latent=latent, tq=tq),
        grid=(n_batch, nq),
        in_specs=in_specs,
        out_specs=out_specs,
        out_shape=out_shape,
        scratch_shapes=scratch,
        compiler_params=_params("parallel", "parallel"),
        name="attention_latent" if latent else "attention_context",
    )(*args)


def _route_kernel(afft_ref, tri_ref, slot_ref, *, cap):
    a = afft_ref[...]
    capf = float(cap)
    count_ge = lambda t: jnp.sum(jnp.where(a >= t, 1.0, 0.0), axis=1, keepdims=True)
    p = jnp.full((a.shape[0], 1), 2.0, F32)
    for step in EXP_STEPS:
        cand = p * (2.0 ** -step)
        p = jnp.where(count_ge(cand) < capf, cand, p)
    lo0 = p * 0.5

    def refine(_, carry):
        lo, delta = carry
        cand = lo + delta
        return jnp.where(count_ge(cand) >= capf, cand, lo), delta * 0.5

    thr, _ = lax.fori_loop(0, MANTISSA_STEPS, refine, (lo0, lo0 * 0.5))
    above = a > thr
    tied = a == thr
    need = capf - jnp.sum(jnp.where(above, 1.0, 0.0), axis=1, keepdims=True)
    tri = tri_ref[...]
    tie_rank = _dot(jnp.where(tied, 1.0, 0.0).astype(BF16), tri)
    sel = jnp.logical_or(above, jnp.logical_and(tied, tie_rank < need))
    slot = _dot(jnp.where(sel, 1.0, 0.0).astype(BF16), tri)
    slot_ref[...] = jnp.where(sel, slot, -1.0).astype(jnp.int32)


def _route(afft2d, tri, cap):
    rows, n_tok = afft2d.shape
    return pl.pallas_call(
        functools.partial(_route_kernel, cap=cap),
        grid=(1,),
        in_specs=[pl.BlockSpec((rows, n_tok), lambda i: (0, 0)),
                  pl.BlockSpec((n_tok, n_tok), lambda i: (0, 0))],
        out_specs=pl.BlockSpec((rows, n_tok), lambda i: (0, 0)),
        out_shape=jax.ShapeDtypeStruct((rows, n_tok), jnp.int32),
        compiler_params=_params("arbitrary"),
        name="route",
    )(afft2d, tri)


def _one_hot_rows(slot_row, cap):
    return slot_row == lax.broadcasted_iota(jnp.int32, (cap, slot_row.shape[1]), 0)


def _gather_kernel(slot_ref, afft_ref, h_ref, xg_ref, g_ref, *, cap):
    h = h_ref[...]
    onehots = []
    for e in range(N_EXPERTS):
        onehot = _one_hot_rows(slot_ref[e:e + 1, :], cap)
        onehots.append(jnp.where(onehot, 1.0, 0.0).astype(BF16))
        g = jnp.sum(jnp.where(onehot, afft_ref[0, e:e + 1, :], 0.0), axis=1, keepdims=True)
        g_ref[e] = jnp.broadcast_to(g, (cap, LANES))
    x = _dot(jnp.concatenate(onehots, axis=0), h)
    xg_ref[...] = x.astype(BF16).reshape(N_EXPERTS, cap, D_MODEL)


def _gather(slot2d, afft, h2, n_batch, n_tok, cap):
    return pl.pallas_call(
        functools.partial(_gather_kernel, cap=cap),
        grid=(n_batch,),
        in_specs=[
            pl.BlockSpec((N_EXPERTS, n_tok), lambda b: (b, 0)),
            pl.BlockSpec((1, N_EXPERTS, n_tok), lambda b: (b, 0, 0)),
            pl.BlockSpec((n_tok, D_MODEL), lambda b: (b, 0)),
        ],
        out_specs=[
            pl.BlockSpec((N_EXPERTS, cap, D_MODEL), lambda b: (0, b, 0)),
            pl.BlockSpec((N_EXPERTS, cap, LANES), lambda b: (0, b, 0)),
        ],
        out_shape=[
            jax.ShapeDtypeStruct((N_EXPERTS, n_batch * cap, D_MODEL), BF16),
            jax.ShapeDtypeStruct((N_EXPERTS, n_batch * cap, LANES), F32),
        ],
        compiler_params=_params("parallel"),
        name="gather",
    )(slot2d, afft, h2)


def _ffn_kernel(xc_ref, xl_ref, gc_ref, gl_ref, wg_ref, wu_ref, wd_ref, yc_ref, yl_ref):
    wg = wg_ref[0].astype(BF16)
    wu = wu_ref[0].astype(BF16)
    wd = wd_ref[0].astype(BF16)
    for x_ref, g_ref, y_ref in ((xc_ref, gc_ref, yc_ref), (xl_ref, gl_ref, yl_ref)):
        for r in range(0, x_ref.shape[1], FFN_ROW_TILE):
            x = x_ref[0, r:r + FFN_ROW_TILE, :]
            act = (_silu(_dot(x, wg)) * _dot(x, wu)).astype(BF16)
            g = jnp.tile(g_ref[0, r:r + FFN_ROW_TILE, :], (1, D_MODEL // LANES))
            y_ref[0, r:r + FFN_ROW_TILE, :] = (_dot(act, wd) * g).astype(BF16)


def _ffn(xc, xl, gc, gl, w_gate, w_up, w_down):
    rows_c, rows_l = xc.shape[1], xl.shape[1]
    assert rows_c % FFN_ROW_TILE == 0 and rows_l % FFN_ROW_TILE == 0
    d_ff = w_gate.shape[2]
    per_expert = lambda e: (e, 0, 0)
    return pl.pallas_call(
        _ffn_kernel,
        grid=(N_EXPERTS,),
        in_specs=[
            pl.BlockSpec((1, rows_c, D_MODEL), per_expert),
            pl.BlockSpec((1, rows_l, D_MODEL), per_expert),
            pl.BlockSpec((1, rows_c, LANES), per_expert),
            pl.BlockSpec((1, rows_l, LANES), per_expert),
            pl.BlockSpec((1, D_MODEL, d_ff), per_expert),
            pl.BlockSpec((1, D_MODEL, d_ff), per_expert),
            pl.BlockSpec((1, d_ff, D_MODEL), per_expert),
        ],
        out_specs=[
            pl.BlockSpec((1, rows_c, D_MODEL), per_expert),
            pl.BlockSpec((1, rows_l, D_MODEL), per_expert),
        ],
        out_shape=[jax.ShapeDtypeStruct(xc.shape, BF16), jax.ShapeDtypeStruct(xl.shape, BF16)],
        compiler_params=_params("arbitrary"),
        name="expert_ffn",
    )(xc, xl, gc, gl, w_gate, w_up, w_down)


def _scatter_kernel(slot_ref, y_ref, x1_ref, gt2_ref, gpost_ref, out_ref, *, cap):
    onehot = jnp.concatenate(
        [jnp.where(_one_hot_rows(slot_ref[e:e + 1, :], cap), 1.0, 0.0).astype(BF16)
         for e in range(N_EXPERTS)], axis=0)
    y = y_ref[...].reshape(N_EXPERTS * cap, D_MODEL)
    moe = _dot_tn(onehot, y)
    out_ref[...] = x1_ref[...] + gt2_ref[0] * (_rms(moe) * gpost_ref[...])


def _scatter(slot2d, y, x1, mod3, mod_row, n_batch, n_tok, cap, gpost):
    return pl.pallas_call(
        functools.partial(_scatter_kernel, cap=cap),
        grid=(n_batch,),
        in_specs=[
            pl.BlockSpec((N_EXPERTS, n_tok), lambda b: (b, 0)),
            pl.BlockSpec((N_EXPERTS, cap, D_MODEL), lambda b: (0, b, 0)),
            pl.BlockSpec((n_tok, D_MODEL), lambda b: (b, 0)),
            pl.BlockSpec((1, 1, D_MODEL), lambda b: (mod_row(b) * N_MOD + 5, 0, 0)),
            pl.BlockSpec((1, D_MODEL), lambda b: (0, 0)),
        ],
        out_specs=pl.BlockSpec((n_tok, D_MODEL), lambda b: (b, 0)),
        out_shape=jax.ShapeDtypeStruct((n_batch * n_tok, D_MODEL), F32),
        compiler_params=_params("parallel"),
        name="scatter",
    )(slot2d, y, x1, mod3, gpost)


def _pair_heads(w, axis):
    shape = w.shape
    w = w.reshape(shape[:axis] + (2, 4, HEAD_DIM) + shape[axis + 1:])
    w = jnp.swapaxes(w, axis, axis + 1)
    return w.reshape(shape)


def _rope_tables(n_tok):
    f32 = np.float32
    half = HEAD_DIM // 4
    freqs = f32(ROPE_THETA) ** (-np.arange(half, dtype=f32) / f32(half))
    rows = np.repeat(np.arange(n_tok // GRID_W, dtype=f32), GRID_W)
    cols = np.tile(np.arange(GRID_W, dtype=f32), n_tok // GRID_W)
    d = np.arange(HEAD_DIM)
    pos = np.where((d // (2 * half))[None, :] == 0, rows[:, None], cols[:, None])
    ang = (pos * freqs[d % half][None, :]).astype(f32)
    sign = np.where((d % (2 * half)) < half, f32(-1.0), f32(1.0))
    cos = np.tile(np.cos(ang), (1, LANES // HEAD_DIM))
    sin = np.tile(np.sin(ang) * sign[None, :], (1, LANES // HEAD_DIM))
    return cos.astype(f32), sin.astype(f32)


def _prefix_matrix(n):
    r = np.arange(n)
    return (r[:, None] < r[None, :]).astype(BF16)


def kernel(x_prompt, x_sample, cache_a_k, cache_a_v, cache_b_k, cache_b_v, c, c_ctx, w_ada, b_ada,
           g_pre_attn, w_in, g_qa, g_ka, sink_b, w_out, g_post_attn, g_pre_ffn, w_router, w_gate,
           w_up, w_down, g_post_ffn):
    n_ctx, t_ctx, _ = x_prompt.shape
    n_lat, t_lat, _ = x_sample.shape
    assert w_ada.shape[0] == 1, "single-layer step"
    assert n_lat + 1 <= MOD_ROWS

    w_in0 = w_in[0]
    c1, c2, c3, c4, c5 = Q_W, Q_W + KV_W, Q_W + 2 * KV_W, 2 * Q_W + 2 * KV_W, 2 * Q_W + 3 * KV_W
    w_in_p = jnp.concatenate(
        [_pair_heads(w_in0[:, :c1], 1), _pair_heads(w_in0[:, c3:c4], 1),
         w_in0[:, c1:c3], w_in0[:, c4:]], axis=1).astype(BF16)
    w_out0 = w_out[0]
    w_out_p = jnp.stack(
        [_pair_heads(w_out0[:Q_W], 0).reshape(Q_W // LANES, LANES, D_MODEL),
         _pair_heads(w_out0[Q_W:], 0).reshape(Q_W // LANES, LANES, D_MODEL)],
        axis=1).reshape(2 * Q_W, D_MODEL).astype(BF16)
    wr0 = w_router[0]
    wr_hi = wr0.astype(BF16)
    wr_lo = (wr0 - wr_hi.astype(F32)).astype(BF16)
    wr_split = jnp.pad(jnp.concatenate([wr_hi, wr_lo], axis=1),
                       ((0, 0), (0, LANES - 2 * N_EXPERTS)))
    gq = jnp.tile(g_qa[0], N_HEADS)[None, :]
    gk = jnp.tile(g_ka[0], N_KV)[None, :]
    blk = np.arange(2 * LANES) // HEAD_DIM
    bd = (blk[:, None] == blk[None, :]).astype(BF16)
    sink = sink_b[0].astype(F32)
    cond = jnp.concatenate(
        [c, c_ctx[None, :], jnp.zeros((MOD_ROWS - n_lat - 1, D_MODEL), F32)], axis=0)

    mod = _modulation(cond, w_ada[0], b_ada[0][None, :])
    mod3 = mod.reshape(MOD_ROWS * N_MOD, 1, D_MODEL)

    xc = x_prompt.reshape(n_ctx * t_ctx, D_MODEL)
    xl = x_sample.reshape(n_lat * t_lat, D_MODEL)
    gpre1, gpost1 = g_pre_attn[0][None, :], g_post_attn[0][None, :]
    gpre2, gpost2 = g_pre_ffn[0][None, :], g_post_ffn[0][None, :]

    ctx_row = lambda b: n_lat
    lat_row = lambda b: b

    qc, kc, vtc, ka, va, kb, vb = _projection(
        xc, mod3, ctx_row, t_ctx, t_ctx, w_in_p, gpre1, gq, gk, bd, None)
    x1c, h2c, afftc = _attention(
        sink, qc, kc, vtc, xc, mod3, ctx_row, n_ctx, t_ctx, t_ctx, gpost1, gpre2, w_out_p,
        wr_split, None)

    ql, kl, vtl = _projection(
        xl, mod3, lat_row, t_lat, LATENT_PROJ_TILE, w_in_p, gpre1, gq, gk, bd, _rope_tables(t_lat))
    caches = [jnp.transpose(t[:, 0], (0, 2, 3, 1)).reshape(n_lat, KV_W, t.shape[2])
              for t in (cache_a_k, cache_a_v, cache_b_k, cache_b_v)]
    x1l, h2l, afftl = _attention(
        sink, ql, kl, vtl, xl, mod3, lat_row, n_lat, t_lat, LATENT_Q_TILE, gpost1, gpre2, w_out_p,
        wr_split, caches)

    cap_c = CAPACITY_FACTOR * t_ctx // N_EXPERTS
    cap_l = CAPACITY_FACTOR * t_lat // N_EXPERTS
    slot_c = _route(afftc.reshape(n_ctx * N_EXPERTS, t_ctx), _prefix_matrix(t_ctx), cap_c)
    slot_l = _route(afftl.reshape(n_lat * N_EXPERTS, t_lat), _prefix_matrix(t_lat), cap_l)
    xgc, gc = _gather(slot_c, afftc, h2c, n_ctx, t_ctx, cap_c)
    xgl, gl = _gather(slot_l, afftl, h2l, n_lat, t_lat, cap_l)
    yc, yl = _ffn(xgc, xgl, gc, gl, w_gate[0], w_up[0], w_down[0])
    y_prompt = _scatter(slot_c, yc, x1c, mod3, ctx_row, n_ctx, t_ctx, cap_c, gpost2)
    y_sample = _scatter(slot_l, yl, x1l, mod3, lat_row, n_lat, t_lat, cap_l, gpost2)

    new_cache = lambda t: jnp.transpose(
        t.reshape(n_ctx, 1, N_KV, HEAD_DIM, t_ctx), (0, 1, 4, 2, 3))
    return (y_prompt.reshape(x_prompt.shape), y_sample.reshape(x_sample.shape),
            new_cache(ka), new_cache(va), new_cache(kb), new_cache(vb))
```

```python
import functools

import jax
import jax.numpy as jnp
import numpy as np
from jax import lax
from jax.experimental import pallas as pl
from jax.experimental.pallas import tpu as pltpu

F32 = jnp.float32
BF16 = jnp.bfloat16

D_MODEL = 1024
HEAD_DIM = 64
N_HEADS = 8
N_KV = 2
Q_W = N_HEADS * HEAD_DIM
KV_W = N_KV * HEAD_DIM
IN_W = 2 * (Q_W + 2 * KV_W)
GRID_W = 64
BLOCK = 128
WINDOW = 128
ROPE_THETA = 10000.0
N_EXPERTS = 16
CAPACITY_FACTOR = 2
N_MOD = 6
EPS = 1e-6
NEG_INF = -1e30
LOG2E = 1.4426950408889634
LANES = 128
MOD_ROWS = 16
VMEM_LIMIT = 56 * 1024 * 1024
FFN_ROW_TILE = 512
LATENT_Q_TILE = 256
ATTN_SUB_TILE = 256
PROJ_TILE = 1024
PROJ_SUB_TILE = 256
SCORE_LOOKAHEAD = 4

EXP_STEPS = (64, 32, 16, 8, 4, 2, 1)
MANTISSA_STEPS = 36


def _dot(a, b):
    return jnp.dot(a, b, preferred_element_type=F32)


def _dot_nt(a, b):
    return lax.dot_general(a, b, (((1,), (1,)), ((), ())), preferred_element_type=F32)


def _dot_tn(a, b):
    return lax.dot_general(a, b, (((0,), (0,)), ((), ())), preferred_element_type=F32)


def _split_bf16(x):
    hi = x.astype(BF16)
    lo = (x - hi.astype(F32)).astype(BF16)
    return hi, lo


def _rms(x):
    return x * lax.rsqrt(jnp.mean(x * x, axis=-1, keepdims=True) + EPS)


def _silu(x):
    return x * (1.0 / (1.0 + jnp.exp(-x)))


def _params(*sem):
    return pltpu.CompilerParams(dimension_semantics=sem, vmem_limit_bytes=VMEM_LIMIT)


def _mod_kernel(cond_ref, w_ref, b_ref, out_ref):
    a_hi, a_lo = _split_bf16(_silu(cond_ref[...]))
    w_hi, w_lo = _split_bf16(w_ref[...])
    out_ref[...] = _dot(a_hi, w_hi) + _dot(a_hi, w_lo) + _dot(a_lo, w_hi) + b_ref[...]


def _modulation(cond, w_ada, b_ada):
    n = w_ada.shape[1]
    tn = 1536
    return pl.pallas_call(
        _mod_kernel,
        grid=(n // tn,),
        in_specs=[
            pl.BlockSpec((MOD_ROWS, D_MODEL), lambda j: (0, 0)),
            pl.BlockSpec((D_MODEL, tn), lambda j: (0, j)),
            pl.BlockSpec((1, tn), lambda j: (0, j)),
        ],
        out_specs=pl.BlockSpec((MOD_ROWS, tn), lambda j: (0, j)),
        out_shape=jax.ShapeDtypeStruct((MOD_ROWS, n), F32),
        compiler_params=_params("arbitrary"),
        name="modulation",
    )(cond, w_ada, b_ada)


def _rotate_pairs(x):
    lane = lax.broadcasted_iota(jnp.int32, x.shape, 1)
    return jnp.where((lane & 31) < 16, pltpu.roll(x, LANES - 16, 1), pltpu.roll(x, 16, 1))


def _proj_kernel(x_ref, sh_ref, sc_ref, gpre_ref, w_ref, gq_ref, gk_ref, bd_ref, *rest, rope):
    if rope:
        cos_ref, sin_ref, q_ref, k_ref, vt_ref = rest
    else:
        q_ref, k_ref, vt_ref, ka_ref, va_ref, kb_ref, vb_ref = rest
    tm = x_ref.shape[0]
    sub = min(tm, PROJ_SUB_TILE, vt_ref.shape[2])
    gain = gpre_ref[...] * (1.0 + sc_ref[0])
    shift = sh_ref[0]
    bd = bd_ref[...]
    scale = LOG2E * HEAD_DIM ** -0.5
    gq = gq_ref[...] * scale
    gk = gk_ref[...]
    k0 = 2 * Q_W

    def head_norm(t, g):
        ss = _dot((t * t).astype(BF16), bd[: t.shape[1], : t.shape[1]])
        return t * (lax.rsqrt(ss * (1.0 / HEAD_DIM) + EPS) * g)

    for r in range(0, tm, sub):
        rows = slice(r, r + sub)
        h = _rms(x_ref[rows, :]) * gain + shift
        p = _dot(h.astype(BF16), w_ref[...])
        qa = [head_norm(p[:, c:c + 2 * LANES], gq[:, c:c + 2 * LANES]) for c in (0, 2 * LANES)]
        qa = [t[:, c:c + LANES] for t in qa for c in (0, LANES)]
        qb = [p[:, Q_W + c:Q_W + c + LANES] for c in range(0, Q_W, LANES)]
        ka = head_norm(p[:, k0:k0 + KV_W], gk)
        va = p[:, k0 + KV_W:k0 + 2 * KV_W]
        kb = p[:, k0 + 2 * KV_W:k0 + 3 * KV_W]
        vb = p[:, k0 + 3 * KV_W:k0 + 4 * KV_W]
        if rope:
            cos = cos_ref[rows, :]
            sin = sin_ref[rows, :]
            rot = lambda t, c, s: t * c + _rotate_pairs(t) * s
            ka_r, kb_r = rot(ka, cos, sin), rot(kb, cos, sin)
            qa = [rot(t, cos, sin) for t in qa]
            cos_q, sin_q = cos * scale, sin * scale
            qb = [rot(t, cos_q, sin_q) for t in qb]
        else:
            ka_r, kb_r = ka, kb
            qb = [t * scale for t in qb]
        q_ref[rows, :] = jnp.concatenate(qa + qb, axis=1).astype(BF16)
        k_ref[rows, :] = jnp.concatenate([ka_r, kb_r], axis=1).astype(BF16)
        v_t = jnp.concatenate([va, vb], axis=1).T
        bi, toks = r // vt_ref.shape[2], slice(r % vt_ref.shape[2], r % vt_ref.shape[2] + sub)
        vt_ref[bi, :, toks] = v_t.astype(BF16)
        if not rope:
            k_t = jnp.concatenate([ka, kb], axis=1).T
            ka_ref[bi, :, toks] = k_t[:KV_W]
            kb_ref[bi, :, toks] = k_t[KV_W:]
            va_ref[bi, :, toks] = v_t[:KV_W]
            vb_ref[bi, :, toks] = v_t[KV_W:]


def _projection(x2d, mod3, mod_row, rows_per_batch, tm, w_in_p, gpre, gq, gk, bd, rope_tabs):
    n_rows = x2d.shape[0]
    rope = rope_tabs is not None
    batches_per_tile = max(1, tm // rows_per_batch)
    tiles_per_batch = max(1, rows_per_batch // tm)
    tok_blk = min(tm, rows_per_batch)
    assert batches_per_tile == 1 or (not rope and mod_row(0) == mod_row(1))
    fm_map = lambda i: (i // tiles_per_batch, 0, i % tiles_per_batch)

    def mod_spec(j):
        return pl.BlockSpec(
            (1, 1, D_MODEL),
            lambda i: (mod_row(i * batches_per_tile // tiles_per_batch) * N_MOD + j, 0, 0))

    const = lambda shape: pl.BlockSpec(shape, lambda i: (0, 0))
    in_specs = [
        pl.BlockSpec((tm, D_MODEL), lambda i: (i, 0)),
        mod_spec(0), mod_spec(1),
        const((1, D_MODEL)), const((D_MODEL, IN_W)), const((1, Q_W)), const((1, KV_W)),
        const((2 * LANES, 2 * LANES)),
    ]
    args = [x2d, mod3, mod3, gpre, w_in_p, gq, gk, bd]
    out_specs = [pl.BlockSpec((tm, 2 * Q_W), lambda i: (i, 0)),
                 pl.BlockSpec((tm, 2 * KV_W), lambda i: (i, 0)),
                 pl.BlockSpec((batches_per_tile, 2 * KV_W, tok_blk), fm_map)]
    out_shape = [jax.ShapeDtypeStruct((n_rows, 2 * Q_W), BF16),
                 jax.ShapeDtypeStruct((n_rows, 2 * KV_W), BF16),
                 jax.ShapeDtypeStruct((n_rows // rows_per_batch, 2 * KV_W, rows_per_batch), BF16)]
    if rope:
        tab_spec = pl.BlockSpec((tm, LANES), lambda i: (i % tiles_per_batch, 0))
        in_specs += [tab_spec, tab_spec]
        args += list(rope_tabs)
    else:
        out_specs += [pl.BlockSpec((batches_per_tile, KV_W, tok_blk), fm_map)] * 4
        out_shape += [jax.ShapeDtypeStruct(
            (n_rows // rows_per_batch, KV_W, rows_per_batch), F32)] * 4
    return pl.pallas_call(
        functools.partial(_proj_kernel, rope=rope),
        grid=(n_rows // tm,),
        in_specs=in_specs,
        out_specs=out_specs,
        out_shape=out_shape,
        compiler_params=_params("parallel"),
        name="projection_latent" if rope else "projection_context",
    )(*args)


def _ring_size(n_chains_per_step):
    return min(n_chains_per_step, SCORE_LOOKAHEAD + 2)


def _band_keys(tq):
    assert WINDOW % LANES == 0 and tq % LANES == 0
    return tq + 2 * WINDOW


def _pair_scores(chunk, keys, masks, sink_row, s_ref):
    lane = lax.broadcasted_iota(jnp.int32, chunk.shape, 1)
    zero = jnp.zeros_like(chunk)
    qpair = jnp.concatenate(
        [jnp.where(lane < HEAD_DIM, chunk, zero), jnp.where(lane >= HEAD_DIM, chunk, zero)], axis=0)
    off, mx = 0, sink_row
    for k, m in zip(keys, masks):
        s = _dot_nt(k, qpair)
        s = s if m is None else jnp.where(m, s, NEG_INF)
        s_ref[off:off + k.shape[0], :] = s
        part = jnp.max(s, axis=0, keepdims=True)
        mx = part if mx is None else jnp.maximum(mx, part)
        off += k.shape[0]
    return mx


def _pair_softmax(s_ref, mx, sink_row, e_ref):
    e = jnp.exp2(s_ref[...] - mx)
    denom = jnp.sum(e, axis=0, keepdims=True)
    if sink_row is not None:
        denom = denom + jnp.exp2(sink_row - mx)
    e_ref[...] = e.astype(BF16)
    return denom


def _pair_output(e_ref, values_t, denom, tq):
    off, o_t = 0, None
    for vt in values_t:
        part = _dot(vt, e_ref[off:off + vt.shape[1], :])
        o_t = part if o_t is None else o_t + part
        off += vt.shape[1]
    o_t = o_t * (1.0 / denom)
    feat = lax.broadcasted_iota(jnp.int32, (LANES, tq), 0)
    return jnp.where(feat < HEAD_DIM, o_t[:, :tq], o_t[:, tq:]).T


def _attn_kernel(sink_ref, q_ref, k_ref, vt_ref, x_ref, gt1_ref, sh2_ref, sc2_ref, gpost_ref,
                 gpre2_ref, wout_ref, wr_ref, *rest, latent, tq, sub):
    n_pairs = Q_W // LANES
    n_chains = 2 * n_pairs
    n_sub = tq // sub
    n_bufs = _ring_size(n_sub * n_chains)
    s_refs, e_refs = rest[-2 * n_bufs:-n_bufs], rest[-n_bufs:]
    rest = rest[:-2 * n_bufs]
    if latent:
        cak_ref, cav_ref, cbk_ref, cbv_ref, x1_ref, h2_ref, afft_ref = rest
    else:
        x1_ref, h2_ref, afft_ref = rest
    col = lax.broadcasted_iota(jnp.int32, (1, 2 * sub), 1)
    if latent:
        n_tok = k_ref.shape[0]
        band = _band_keys(sub)
        cak, cav = cak_ref[0].T.astype(BF16), cav_ref[0].astype(BF16)
        cbk, cbv = cbk_ref[0].T.astype(BF16), cbv_ref[0].astype(BF16)

    def operands(u):
        if not latent:
            return (([k_ref[:, :KV_W]], [vt_ref[0, :KV_W, :]], [None]),
                    ([k_ref[:, KV_W:]], [vt_ref[0, KV_W:, :]], [None]))
        t0 = pl.program_id(1) * tq + u * sub
        start = pl.multiple_of(jnp.clip(t0 - WINDOW, 0, n_tok - band), LANES)
        kpos = start + lax.broadcasted_iota(jnp.int32, (band, 2 * sub), 0)
        qpos = t0 + (lax.broadcasted_iota(jnp.int32, (band, 2 * sub), 1) & (sub - 1))
        return (([cak, k_ref[:, :KV_W]], [cav, vt_ref[0, :KV_W, :]], [None, None]),
                ([cbk, k_ref[pl.ds(start, band), KV_W:]],
                 [cbv, vt_ref[0, KV_W:, pl.ds(start, band)]],
                 [None, jnp.abs(kpos - qpos) <= WINDOW]))

    ops = [operands(u) for u in range(n_sub)]

    def chain(u, c):
        keys, vals, masks = ops[u][c // n_pairs]
        sink_row = None
        if c >= n_pairs:
            j = c - n_pairs
            sink_row = jnp.where(col < sub, sink_ref[j], sink_ref[j + n_pairs]) * LOG2E
        buf = (u * n_chains + 2 * (c % n_pairs) + c // n_pairs) % n_bufs
        n_keys = sum(k.shape[0] for k in keys)
        return (keys, vals, masks, sink_row,
                s_refs[buf].at[0:n_keys, :], e_refs[buf].at[0:n_keys, :])

    col_max = {}

    def scores(u, c):
        keys, _, masks, sink_row, s_ref, _ = chain(u, c)
        col_max[u, c] = _pair_scores(
            q_ref[u * sub:(u + 1) * sub, c * LANES:(c + 1) * LANES], keys, masks, sink_row, s_ref)

    def epilogue(u, proj):
        rows = slice(u * sub, (u + 1) * sub)
        x1 = x_ref[rows, :] + gt1_ref[0] * (_rms(proj) * gpost_ref[...])
        x1_ref[rows, :] = x1
        h2 = (_rms(x1) * gpre2_ref[...]) * (1.0 + sc2_ref[0]) + sh2_ref[0]
        h2_ref[rows, :] = h2.astype(BF16)
        h_hi, h_lo = _split_bf16(h2)
        wr = wr_ref[...]
        p_hi = _dot(h_hi, wr)
        logits = p_hi + pltpu.roll(p_hi, LANES - N_EXPERTS, 1) + _dot(h_lo, wr)
        lane = lax.broadcasted_iota(jnp.int32, logits.shape, 1)
        logits = jnp.where(lane < N_EXPERTS, logits, NEG_INF)
        ex = jnp.exp(logits - jnp.max(logits, axis=-1, keepdims=True))
        aff = ex * (1.0 / jnp.sum(ex, axis=-1, keepdims=True))
        afft_ref[0, :, rows] = aff.T[:N_EXPERTS]

    order = [(u, c) for u in range(n_sub) for j in range(n_pairs) for c in (j, j + n_pairs)]
    chunks = {}
    proj = None
    for u, c in order[:SCORE_LOOKAHEAD]:
        scores(u, c)
    for i, (u, c) in enumerate(order):
        if i + SCORE_LOOKAHEAD < len(order):
            scores(*order[i + SCORE_LOOKAHEAD])
        _, vals, _, sink_row, s_ref, e_ref = chain(u, c)
        denom = _pair_softmax(s_ref, col_max[u, c], sink_row, e_ref)
        chunks[c] = _pair_output(e_ref, vals, denom, sub)
        if c >= n_pairs:
            j = c - n_pairs
            o_pair = jnp.concatenate([chunks[j], chunks[c]], axis=1).astype(BF16)
            part = _dot(o_pair, wout_ref[2 * j * LANES:2 * (j + 1) * LANES, :])
            proj = part if proj is None else proj + part
        if c == n_chains - 1:
            epilogue(u, proj)
            proj = None


def _attention(sink_p, q, k, vt, x2d, mod3, mod_row, n_batch, n_tok, tq, gpost, gpre2, wout_p,
               wr_split, caches):
    latent = caches is not None
    nq = n_tok // tq
    row = lambda b, i: (b * nq + i, 0)

    def mod_spec(j):
        return pl.BlockSpec((1, 1, D_MODEL), lambda b, i: (mod_row(b) * N_MOD + j, 0, 0))

    const = lambda shape: pl.BlockSpec(shape, lambda b, i: (0, 0))
    in_specs = [
        pl.BlockSpec(memory_space=pltpu.SMEM),
        pl.BlockSpec((tq, 2 * Q_W), row),
        pl.BlockSpec((n_tok, 2 * KV_W), lambda b, i: (b, 0)),
        pl.BlockSpec((1, 2 * KV_W, n_tok), lambda b, i: (b, 0, 0)),
        pl.BlockSpec((tq, D_MODEL), row),
        mod_spec(2), mod_spec(3), mod_spec(4),
        const((1, D_MODEL)), const((1, D_MODEL)),
        const((2 * Q_W, D_MODEL)), const((D_MODEL, LANES)),
    ]
    args = [sink_p, q, k, vt, x2d, mod3, mod3, mod3, gpost, gpre2, wout_p, wr_split]
    if latent:
        cache_spec = pl.BlockSpec((1,) + caches[0].shape[1:], lambda b, i: (b, 0, 0))
        in_specs += [cache_spec] * 4
        args += list(caches)
    out_specs = [
        pl.BlockSpec((tq, D_MODEL), row),
        pl.BlockSpec((tq, D_MODEL), row),
        pl.BlockSpec((1, N_EXPERTS, tq), lambda b, i: (b, 0, i)),
    ]
    out_shape = [
        jax.ShapeDtypeStruct((n_batch * n_tok, D_MODEL), F32),
        jax.ShapeDtypeStruct((n_batch * n_tok, D_MODEL), BF16),
        jax.ShapeDtypeStruct((n_batch, N_EXPERTS, n_tok), F32),
    ]
    sub = ATTN_SUB_TILE
    n_ctx_keys = caches[0].shape[2] if latent else 0
    keys_a = n_ctx_keys + n_tok
    keys_b = n_ctx_keys + (_band_keys(sub) if latent else n_tok)
    n_pairs = Q_W // LANES
    n_bufs = _ring_size(2 * n_pairs * (tq // sub))
    assert keys_a >= keys_b
    scratch = ([pltpu.VMEM((keys_a, 2 * sub), F32)] * n_bufs
               + [pltpu.VMEM((keys_a, 2 * sub), BF16)] * n_bufs)
    return pl.pallas_call(
        functools.partial(_attn_kernel, latent=latent, tq=tq, sub=sub),
        grid=(n_batch, nq),
        in_specs=in_specs,
        out_specs=out_specs,
        out_shape=out_shape,
        scratch_shapes=scratch,
        compiler_params=_params("parallel", "parallel"),
        name="attention_latent" if latent else "attention_context",
    )(*args)


def _route_kernel(afft_ref, tri_ref, slot_ref, *, cap):
    a = afft_ref[...]
    capf = float(cap)
    count_ge = lambda t: jnp.sum(jnp.where(a >= t, 1.0, 0.0), axis=1, keepdims=True)
    p = jnp.full((a.shape[0], 1), 2.0, F32)
    for step in EXP_STEPS:
        cand = p * (2.0 ** -step)
        p = jnp.where(count_ge(cand) < capf, cand, p)
    lo0 = p * 0.5

    def refine(_, carry):
        lo, delta = carry
        cand = lo + delta
        return jnp.where(count_ge(cand) >= capf, cand, lo), delta * 0.5

    thr, _ = lax.fori_loop(0, MANTISSA_STEPS, refine, (lo0, lo0 * 0.5))
    above = a > thr
    tied = a == thr
    need = capf - jnp.sum(jnp.where(above, 1.0, 0.0), axis=1, keepdims=True)
    tri = tri_ref[...]
    tie_rank = _dot(jnp.where(tied, 1.0, 0.0).astype(BF16), tri)
    sel = jnp.logical_or(above, jnp.logical_and(tied, tie_rank < need))
    slot = _dot(jnp.where(sel, 1.0, 0.0).astype(BF16), tri)
    slot_ref[...] = jnp.where(sel, slot, -1.0).astype(jnp.int32)


def _route(afft2d, tri, cap):
    rows, n_tok = afft2d.shape
    return pl.pallas_call(
        functools.partial(_route_kernel, cap=cap),
        grid=(1,),
        in_specs=[pl.BlockSpec((rows, n_tok), lambda i: (0, 0)),
                  pl.BlockSpec((n_tok, n_tok), lambda i: (0, 0))],
        out_specs=pl.BlockSpec((rows, n_tok), lambda i: (0, 0)),
        out_shape=jax.ShapeDtypeStruct((rows, n_tok), jnp.int32),
        compiler_params=_params("arbitrary"),
        name="route",
    )(afft2d, tri)


def _one_hot_rows(slot_row, cap):
    return slot_row == lax.broadcasted_iota(jnp.int32, (cap, slot_row.shape[1]), 0)


def _gather_kernel(slot_ref, afft_ref, h_ref, xg_ref, g_ref, *, cap):
    h = h_ref[...]
    onehots = []
    for e in range(N_EXPERTS):
        onehot = _one_hot_rows(slot_ref[e:e + 1, :], cap)
        onehots.append(jnp.where(onehot, 1.0, 0.0).astype(BF16))
        g = jnp.sum(jnp.where(onehot, afft_ref[0, e:e + 1, :], 0.0), axis=1, keepdims=True)
        g_ref[e] = jnp.broadcast_to(g, (cap, LANES))
    x = _dot(jnp.concatenate(onehots, axis=0), h)
    xg_ref[...] = x.astype(BF16).reshape(N_EXPERTS, cap, D_MODEL)


def _gather(slot2d, afft, h2, n_batch, n_tok, cap):
    return pl.pallas_call(
        functools.partial(_gather_kernel, cap=cap),
        grid=(n_batch,),
        in_specs=[
            pl.BlockSpec((N_EXPERTS, n_tok), lambda b: (b, 0)),
            pl.BlockSpec((1, N_EXPERTS, n_tok), lambda b: (b, 0, 0)),
            pl.BlockSpec((n_tok, D_MODEL), lambda b: (b, 0)),
        ],
        out_specs=[
            pl.BlockSpec((N_EXPERTS, cap, D_MODEL), lambda b: (0, b, 0)),
            pl.BlockSpec((N_EXPERTS, cap, LANES), lambda b: (0, b, 0)),
        ],
        out_shape=[
            jax.ShapeDtypeStruct((N_EXPERTS, n_batch * cap, D_MODEL), BF16),
            jax.ShapeDtypeStruct((N_EXPERTS, n_batch * cap, LANES), F32),
        ],
        compiler_params=_params("parallel"),
        name="gather",
    )(slot2d, afft, h2)


def _ffn_kernel(xc_ref, xl_ref, gc_ref, gl_ref, wg_ref, wu_ref, wd_ref, yc_ref, yl_ref):
    wg = wg_ref[0].astype(BF16)
    wu = wu_ref[0].astype(BF16)
    wd = wd_ref[0].astype(BF16)
    for x_ref, g_ref, y_ref in ((xc_ref, gc_ref, yc_ref), (xl_ref, gl_ref, yl_ref)):
        for r in range(0, x_ref.shape[1], FFN_ROW_TILE):
            x = x_ref[0, r:r + FFN_ROW_TILE, :]
            act = (_silu(_dot(x, wg)) * _dot(x, wu)).astype(BF16)
            g = jnp.tile(g_ref[0, r:r + FFN_ROW_TILE, :], (1, D_MODEL // LANES))
            y_ref[0, r:r + FFN_ROW_TILE, :] = (_dot(act, wd) * g).astype(BF16)


def _ffn(xc, xl, gc, gl, w_gate, w_up, w_down):
    rows_c, rows_l = xc.shape[1], xl.shape[1]
    assert rows_c % FFN_ROW_TILE == 0 and rows_l % FFN_ROW_TILE == 0
    d_ff = w_gate.shape[2]
    per_expert = lambda e: (e, 0, 0)
    return pl.pallas_call(
        _ffn_kernel,
        grid=(N_EXPERTS,),
        in_specs=[
            pl.BlockSpec((1, rows_c, D_MODEL), per_expert),
            pl.BlockSpec((1, rows_l, D_MODEL), per_expert),
            pl.BlockSpec((1, rows_c, LANES), per_expert),
            pl.BlockSpec((1, rows_l, LANES), per_expert),
            pl.BlockSpec((1, D_MODEL, d_ff), per_expert),
            pl.BlockSpec((1, D_MODEL, d_ff), per_expert),
            pl.BlockSpec((1, d_ff, D_MODEL), per_expert),
        ],
        out_specs=[
            pl.BlockSpec((1, rows_c, D_MODEL), per_expert),
            pl.BlockSpec((1, rows_l, D_MODEL), per_expert),
        ],
        out_shape=[jax.ShapeDtypeStruct(xc.shape, BF16), jax.ShapeDtypeStruct(xl.shape, BF16)],
        compiler_params=_params("arbitrary"),
        name="expert_ffn",
    )(xc, xl, gc, gl, w_gate, w_up, w_down)


def _scatter_kernel(slot_ref, y_ref, x1_ref, gt2_ref, gpost_ref, out_ref, *, cap):
    onehot = jnp.concatenate(
        [jnp.where(_one_hot_rows(slot_ref[e:e + 1, :], cap), 1.0, 0.0).astype(BF16)
         for e in range(N_EXPERTS)], axis=0)
    y = y_ref[...].reshape(N_EXPERTS * cap, D_MODEL)
    moe = _dot_tn(onehot, y)
    out_ref[...] = x1_ref[...] + gt2_ref[0] * (_rms(moe) * gpost_ref[...])


def _scatter(slot2d, y, x1, mod3, mod_row, n_batch, n_tok, cap, gpost):
    return pl.pallas_call(
        functools.partial(_scatter_kernel, cap=cap),
        grid=(n_batch,),
        in_specs=[
            pl.BlockSpec((N_EXPERTS, n_tok), lambda b: (b, 0)),
            pl.BlockSpec((N_EXPERTS, cap, D_MODEL), lambda b: (0, b, 0)),
            pl.BlockSpec((n_tok, D_MODEL), lambda b: (b, 0)),
            pl.BlockSpec((1, 1, D_MODEL), lambda b: (mod_row(b) * N_MOD + 5, 0, 0)),
            pl.BlockSpec((1, D_MODEL), lambda b: (0, 0)),
        ],
        out_specs=pl.BlockSpec((n_tok, D_MODEL), lambda b: (b, 0)),
        out_shape=jax.ShapeDtypeStruct((n_batch * n_tok, D_MODEL), F32),
        compiler_params=_params("parallel"),
        name="scatter",
    )(slot2d, y, x1, mod3, gpost)


def _pair_heads(w, axis):
    shape = w.shape
    w = w.reshape(shape[:axis] + (2, 4, HEAD_DIM) + shape[axis + 1:])
    w = jnp.swapaxes(w, axis, axis + 1)
    return w.reshape(shape)


def _rope_tables(n_tok):
    f32 = np.float32
    half = HEAD_DIM // 4
    freqs = f32(ROPE_THETA) ** (-np.arange(half, dtype=f32) / f32(half))
    rows = np.repeat(np.arange(n_tok // GRID_W, dtype=f32), GRID_W)
    cols = np.tile(np.arange(GRID_W, dtype=f32), n_tok // GRID_W)
    d = np.arange(HEAD_DIM)
    pos = np.where((d // (2 * half))[None, :] == 0, rows[:, None], cols[:, None])
    ang = (pos * freqs[d % half][None, :]).astype(f32)
    sign = np.where((d % (2 * half)) < half, f32(-1.0), f32(1.0))
    cos = np.tile(np.cos(ang), (1, LANES // HEAD_DIM))
    sin = np.tile(np.sin(ang) * sign[None, :], (1, LANES // HEAD_DIM))
    return cos.astype(f32), sin.astype(f32)


def _prefix_matrix(n):
    r = np.arange(n)
    return (r[:, None] < r[None, :]).astype(BF16)


def kernel(x_prompt, x_sample, cache_a_k, cache_a_v, cache_b_k, cache_b_v, c, c_ctx, w_ada, b_ada,
           g_pre_attn, w_in, g_qa, g_ka, sink_b, w_out, g_post_attn, g_pre_ffn, w_router, w_gate,
           w_up, w_down, g_post_ffn):
    n_ctx, t_ctx, _ = x_prompt.shape
    n_lat, t_lat, _ = x_sample.shape
    assert w_ada.shape[0] == 1, "single-layer step"
    assert n_lat + 1 <= MOD_ROWS

    w_in0 = w_in[0]
    c1, c2, c3, c4, c5 = Q_W, Q_W + KV_W, Q_W + 2 * KV_W, 2 * Q_W + 2 * KV_W, 2 * Q_W + 3 * KV_W
    w_in_p = jnp.concatenate(
        [_pair_heads(w_in0[:, :c1], 1), _pair_heads(w_in0[:, c3:c4], 1),
         w_in0[:, c1:c3], w_in0[:, c4:]], axis=1).astype(BF16)
    w_out0 = w_out[0]
    w_out_p = jnp.stack(
        [_pair_heads(w_out0[:Q_W], 0).reshape(Q_W // LANES, LANES, D_MODEL),
         _pair_heads(w_out0[Q_W:], 0).reshape(Q_W // LANES, LANES, D_MODEL)],
        axis=1).reshape(2 * Q_W, D_MODEL).astype(BF16)
    wr0 = w_router[0]
    wr_hi = wr0.astype(BF16)
    wr_lo = (wr0 - wr_hi.astype(F32)).astype(BF16)
    wr_split = jnp.pad(jnp.concatenate([wr_hi, wr_lo], axis=1),
                       ((0, 0), (0, LANES - 2 * N_EXPERTS)))
    gq = jnp.tile(g_qa[0], N_HEADS)[None, :]
    gk = jnp.tile(g_ka[0], N_KV)[None, :]
    blk = np.arange(2 * LANES) // HEAD_DIM
    bd = (blk[:, None] == blk[None, :]).astype(BF16)
    sink = sink_b[0].astype(F32)
    cond = jnp.concatenate(
        [c, c_ctx[None, :], jnp.zeros((MOD_ROWS - n_lat - 1, D_MODEL), F32)], axis=0)

    mod = _modulation(cond, w_ada[0], b_ada[0][None, :])
    mod3 = mod.reshape(MOD_ROWS * N_MOD, 1, D_MODEL)

    xc = x_prompt.reshape(n_ctx * t_ctx, D_MODEL)
    xl = x_sample.reshape(n_lat * t_lat, D_MODEL)
    gpre1, gpost1 = g_pre_attn[0][None, :], g_post_attn[0][None, :]
    gpre2, gpost2 = g_pre_ffn[0][None, :], g_post_ffn[0][None, :]

    ctx_row = lambda b: n_lat
    lat_row = lambda b: b

    qc, kc, vtc, ka, va, kb, vb = _projection(
        xc, mod3, ctx_row, t_ctx, PROJ_TILE, w_in_p, gpre1, gq, gk, bd, None)
    x1c, h2c, afftc = _attention(
        sink, qc, kc, vtc, xc, mod3, ctx_row, n_ctx, t_ctx, t_ctx, gpost1, gpre2, w_out_p,
        wr_split, None)

    ql, kl, vtl = _projection(
        xl, mod3, lat_row, t_lat, PROJ_TILE, w_in_p, gpre1, gq, gk, bd, _rope_tables(t_lat))
    caches = [jnp.transpose(t[:, 0], (0, 2, 3, 1)).reshape(n_lat, KV_W, t.shape[2])
              for t in (cache_a_k, cache_a_v, cache_b_k, cache_b_v)]
    x1l, h2l, afftl = _attention(
        sink, ql, kl, vtl, xl, mod3, lat_row, n_lat, t_lat, LATENT_Q_TILE, gpost1, gpre2, w_out_p,
        wr_split, caches)

    cap_c = CAPACITY_FACTOR * t_ctx // N_EXPERTS
    cap_l = CAPACITY_FACTOR * t_lat // N_EXPERTS
    slot_c = _route(afftc.reshape(n_ctx * N_EXPERTS, t_ctx), _prefix_matrix(t_ctx), cap_c)
    slot_l = _route(afftl.reshape(n_lat * N_EXPERTS, t_lat), _prefix_matrix(t_lat), cap_l)
    xgc, gc = _gather(slot_c, afftc, h2c, n_ctx, t_ctx, cap_c)
    xgl, gl = _gather(slot_l, afftl, h2l, n_lat, t_lat, cap_l)
    yc, yl = _ffn(xgc, xgl, gc, gl, w_gate[0], w_up[0], w_down[0])
    y_prompt = _scatter(slot_c, yc, x1c, mod3, ctx_row, n_ctx, t_ctx, cap_c, gpost2)
    y_sample = _scatter(slot_l, yl, x1l, mod3, lat_row, n_lat, t_lat, cap_l, gpost2)

    new_cache = lambda t: jnp.transpose(
        t.reshape(n_ctx, 1, N_KV, HEAD_DIM, t_ctx), (0, 1, 4, 2, 3))
    return (y_prompt.reshape(x_prompt.shape), y_sample.reshape(x_sample.shape),
            new_cache(ka), new_cache(va), new_cache(kb), new_cache(vb))
```
